```python
import jax, jax.numpy as jnp
from jax import lax
import numpy as np

D_MODEL = 1024
BATCH = 4
SEQ = 4096
DEPTH = 1

N_Q_HEADS = 16
N_KV_HEADS = 4
HEAD_DIM = 64
Q_WIDTH = N_Q_HEADS * HEAD_DIM
KV_WIDTH = N_KV_HEADS * HEAD_DIM
WINDOW = 128
BLOCK = 128
GMLP_WIDTH = 1024
GMLP_GROUPS = 8
GMLP_GROUP_DIM = GMLP_WIDTH // GMLP_GROUPS
CHUNK = 128
D_FF = 4 * D_MODEL
LN_EPS = 1e-5
DEEPNORM_ALPHA = (2 * DEPTH) ** 0.25
DEEPNORM_BETA = (8 * DEPTH) ** -0.25
IN_SPLITS = (Q_WIDTH, KV_WIDTH, KV_WIDTH, 2 * GMLP_WIDTH, D_MODEL, D_MODEL)
IN_WIDTH = sum(IN_SPLITS)

kernel_name = "hybrid_swa_gmlp_gated_deepnorm"


def layer_norm(x, g, b):
    xf = x.astype(jnp.float32)
    mu = jnp.mean(xf, axis=-1, keepdims=True)
    var = jnp.mean(jnp.square(xf - mu), axis=-1, keepdims=True)
    y = (xf - mu) * lax.rsqrt(var + LN_EPS)
    return (y * g.astype(jnp.float32) + b.astype(jnp.float32)).astype(x.dtype)


def sliding_window_attention(q, k, v, sinks):
    B, S = q.shape[0], q.shape[1]
    nb = S // BLOCK
    grp = N_Q_HEADS // N_KV_HEADS
    qb = q.reshape(B, nb, BLOCK, N_KV_HEADS, grp, HEAD_DIM)
    kb = k.reshape(B, nb, BLOCK, N_KV_HEADS, HEAD_DIM)
    vb = v.reshape(B, nb, BLOCK, N_KV_HEADS, HEAD_DIM)
    pad = ((0, 0), (1, 0), (0, 0), (0, 0), (0, 0))
    k_band = jnp.concatenate([jnp.pad(kb, pad)[:, :-1], kb], axis=2)
    v_band = jnp.concatenate([jnp.pad(vb, pad)[:, :-1], vb], axis=2)
    scale = HEAD_DIM ** -0.5
    scores = jnp.einsum('bnqhgd,bnshd->bnhgqs', qb, k_band).astype(jnp.float32) * scale
    q_pos = jnp.arange(BLOCK)[:, None] + BLOCK
    k_pos = jnp.arange(2 * BLOCK)[None, :]
    diff = q_pos - k_pos
    band = (diff >= 0) & (diff < WINDOW)
    blk = jnp.arange(nb)[:, None, None]
    valid = (blk > 0) | (k_pos[None] >= BLOCK)
    mask = band[None] & valid
    scores = jnp.where(mask[None, :, None, None], scores, -jnp.inf)
    sink = sinks.astype(jnp.float32).reshape(N_KV_HEADS, grp)[None, None, :, :, None, None]
    m = jnp.maximum(jnp.max(scores, axis=-1, keepdims=True), sink)
    p = jnp.exp(scores - m)
    denom = jnp.sum(p, axis=-1, keepdims=True) + jnp.exp(sink - m)
    probs = (p / denom).astype(v.dtype)
    out = jnp.einsum('bnhgqs,bnshd->bnqhgd', probs, v_band)
    return out.reshape(B, S, Q_WIDTH)


def chunked_spatial_gating(z, ln_g, ln_b, w_s, b_s):
    B, S = z.shape[0], z.shape[1]
    nc = S // CHUNK
    u, v = jnp.split(z, 2, axis=-1)
    v = layer_norm(v, ln_g, ln_b)
    vc = v.reshape(B, nc, CHUNK, GMLP_GROUPS, GMLP_GROUP_DIM)
    causal = jnp.tril(jnp.ones((CHUNK, CHUNK), dtype=bool))
    w = jnp.where(causal[None], w_s, jnp.zeros((), w_s.dtype))
    mixed = jnp.einsum('gts,bnsgd->bntgd', w, vc) + b_s.T[None, None, :, :, None]
    return u * mixed.reshape(B, S, GMLP_WIDTH)


def token_mixer(x, w_in, b_in, sinks, g_ln_g, g_ln_b, g_w_s, g_b_s, w_br_a, w_br_g, w_out):
    B, S = x.shape[0], x.shape[1]
    h = jnp.einsum('bsd,de->bse', x, w_in) + b_in
    offs = np.cumsum(IN_SPLITS)[:-1].tolist()
    q, k, v, z, gate_a, gate_g = jnp.split(h, offs, axis=-1)
    q = q.reshape(B, S, N_Q_HEADS, HEAD_DIM)
    k = k.reshape(B, S, N_KV_HEADS, HEAD_DIM)
    v = v.reshape(B, S, N_KV_HEADS, HEAD_DIM)
    y_a = sliding_window_attention(q, k, v, sinks) @ w_br_a
    y_g = chunked_spatial_gating(jax.nn.gelu(z), g_ln_g, g_ln_b, g_w_s, g_b_s) @ w_br_g
    mix = jax.nn.sigmoid(gate_a) * y_a + jax.nn.sigmoid(gate_g) * y_g
    return mix @ w_out


def squared_relu_mlp(x, w_up, w_down):
    return jnp.square(jax.nn.relu(x @ w_up)) @ w_down


def setup_inputs(seed: int = 0) -> dict:
    key = jax.random.key(seed)
    ks = jax.random.split(key, 18)
    f32 = jnp.float32
    L = DEPTH
    def nrm(k, shape, s):
        return jax.random.normal(k, shape, f32) * s
    return {
        "x": jax.random.normal(ks[0], (BATCH, SEQ, D_MODEL), f32),
        "w_in": nrm(ks[1], (L, D_MODEL, IN_WIDTH), D_MODEL ** -0.5),
        "b_in": nrm(ks[2], (L, IN_WIDTH), 0.02),
        "attn_sinks": nrm(ks[3], (L, N_Q_HEADS), 0.5),
        "gmlp_ln_g": 1.0 + nrm(ks[4], (L, GMLP_WIDTH), 0.05),
        "gmlp_ln_b": nrm(ks[5], (L, GMLP_WIDTH), 0.02),
        "gmlp_w_s": nrm(ks[6], (L, GMLP_GROUPS, CHUNK, CHUNK), CHUNK ** -0.5),
        "gmlp_b_s": 1.0 + nrm(ks[7], (L, GMLP_GROUPS, CHUNK), 0.1),
        "w_branch_attn": nrm(ks[8], (L, Q_WIDTH, D_MODEL), Q_WIDTH ** -0.5),
        "w_branch_gmlp": nrm(ks[9], (L, GMLP_WIDTH, D_MODEL), GMLP_WIDTH ** -0.5),
        "w_out": nrm(ks[10], (L, D_MODEL, D_MODEL), DEEPNORM_BETA * D_MODEL ** -0.5),
        "ln1_g": 1.0 + nrm(ks[11], (L, D_MODEL), 0.05),
        "ln1_b": nrm(ks[12], (L, D_MODEL), 0.02),
        "w_up": nrm(ks[13], (L, D_MODEL, D_FF), D_MODEL ** -0.5),
        "w_down": nrm(ks[14], (L, D_FF, D_MODEL), DEEPNORM_BETA * D_FF ** -0.5),
        "ln2_g": 1.0 + nrm(ks[15], (L, D_MODEL), 0.05),
        "ln2_b": nrm(ks[16], (L, D_MODEL), 0.02),
    }


def reference(x, w_in, b_in, attn_sinks, gmlp_ln_g, gmlp_ln_b, gmlp_w_s, gmlp_b_s,
              w_branch_attn, w_branch_gmlp, w_out, ln1_g, ln1_b, w_up, w_down, ln2_g, ln2_b):
    for l in range(DEPTH):
        mixed = token_mixer(x, w_in[l], b_in[l], attn_sinks[l], gmlp_ln_g[l], gmlp_ln_b[l],
                            gmlp_w_s[l], gmlp_b_s[l], w_branch_attn[l], w_branch_gmlp[l], w_out[l])
        x = layer_norm(DEEPNORM_ALPHA * x + mixed, ln1_g[l], ln1_b[l])
        x = layer_norm(DEEPNORM_ALPHA * x + squared_relu_mlp(x, w_up[l], w_down[l]), ln2_g[l], ln2_b[l])
    return x
```

```python
import jax
import jax.numpy as jnp
from jax.experimental import pallas as pl
from jax.experimental.pallas import tpu as pltpu

D_MODEL = 1024
N_Q_HEADS = 16
N_KV_HEADS = 4
GROUP = N_Q_HEADS // N_KV_HEADS
HEAD_DIM = 64
Q_WIDTH = N_Q_HEADS * HEAD_DIM
KV_WIDTH = N_KV_HEADS * HEAD_DIM
WINDOW = 128
BLOCK = 128
GMLP_WIDTH = 1024
GMLP_GROUPS = 8
GMLP_GROUP_DIM = GMLP_WIDTH // GMLP_GROUPS
D_FF = 4 * D_MODEL
LN_EPS = 1e-5
DEPTH = 1
DEEPNORM_ALPHA = (2 * DEPTH) ** 0.25

OFF_Q = 0
OFF_K = OFF_Q + Q_WIDTH
OFF_V = OFF_K + KV_WIDTH
OFF_Z = OFF_V + KV_WIDTH
OFF_GA = OFF_Z + 2 * GMLP_WIDTH
OFF_GG = OFF_GA + D_MODEL
IN_WIDTH = OFF_GG + D_MODEL

TM_MIX = 256
TM_MLP = 512
FF_CHUNK = 1024
V7X_VMEM_LIMIT_BYTES = 56 * 1024 * 1024

BF16 = jnp.bfloat16
F32 = jnp.float32


def _dot(a, b):
    return jnp.dot(a, b, preferred_element_type=F32)


def _layer_norm(x, g, b):
    mu = jnp.mean(x, axis=-1, keepdims=True)
    xc = x - mu
    var = jnp.mean(xc * xc, axis=-1, keepdims=True)
    return xc * jax.lax.rsqrt(var + LN_EPS) * g + b


def _mixer_kernel(sinks_ref, x_ref, w_in_ref, b_in_ref, lng_ref, lnb_ref, ws_ref, bs_ref,
                  w_bra_ref, w_brg_ref, w_out_ref, ln1g_ref, ln1b_ref,
                  o_ref, kprev_ref, vprev_ref, attn_ref, sg_ref):
    s = pl.program_id(1)
    n_blk = TM_MIX // BLOCK

    x = x_ref[...]
    xb = x.astype(BF16)

    qkv = _dot(xb, w_in_ref[:, OFF_Q:OFF_Z]) + b_in_ref[:, OFF_Q:OFF_Z]
    q = (qkv[:, OFF_Q:OFF_K] * (HEAD_DIM ** -0.5)).astype(BF16)

    k_new = qkv[:, OFF_K:OFF_V].astype(BF16)
    v_new = qkv[:, OFF_V:OFF_Z].astype(BF16)
    no_carry = jnp.zeros((BLOCK, KV_WIDTH), BF16)
    k_prev = jnp.where(s > 0, kprev_ref[s % 2], no_carry)
    v_prev = jnp.where(s > 0, vprev_ref[s % 2], no_carry)
    k_all = jnp.concatenate([k_prev, k_new], axis=0)
    v_all = jnp.concatenate([v_prev, v_new], axis=0)

    q_pos = jax.lax.broadcasted_iota(jnp.int32, (BLOCK, 2 * BLOCK), 0) + BLOCK
    k_pos = jax.lax.broadcasted_iota(jnp.int32, (BLOCK, 2 * BLOCK), 1)
    diff = q_pos - k_pos
    band = (diff >= 0) & (diff < WINDOW)

    for j in range(n_blk):
        first_key = jnp.where(s * n_blk + j > 0, 0, BLOCK)
        mask = band & (k_pos >= first_key)
        kb = k_all[j * BLOCK:(j + 2) * BLOCK, :]
        vb = v_all[j * BLOCK:(j + 2) * BLOCK, :]
        for hq in range(N_Q_HEADS):
            hk = hq // GROUP
            qh = q[j * BLOCK:(j + 1) * BLOCK, hq * HEAD_DIM:(hq + 1) * HEAD_DIM]
            kh = kb[:, hk * HEAD_DIM:(hk + 1) * HEAD_DIM]
            vh = vb[:, hk * HEAD_DIM:(hk + 1) * HEAD_DIM]
            sc = jax.lax.dot_general(qh, kh, (((1,), (1,)), ((), ())),
                                     preferred_element_type=F32)
            sc = jnp.where(mask, sc, -jnp.inf)
            sink = sinks_ref[hq]
            m = jnp.maximum(jnp.max(sc, axis=-1, keepdims=True), sink)
            p = jnp.exp(sc - m)
            denom = jnp.sum(p, axis=-1, keepdims=True) + jnp.exp(sink - m)
            oh = _dot(p.astype(BF16), vh) / denom
            attn_ref[j * BLOCK:(j + 1) * BLOCK, hq * HEAD_DIM:(hq + 1) * HEAD_DIM] = oh.astype(BF16)

    kprev_ref[(s + 1) % 2] = k_new[TM_MIX - BLOCK:, :]
    vprev_ref[(s + 1) % 2] = v_new[TM_MIX - BLOCK:, :]

    z = _dot(xb, w_in_ref[:, OFF_Z:OFF_GA]) + b_in_ref[:, OFF_Z:OFF_GA]
    z = jax.nn.gelu(z, approximate=True)
    u = z[:, :GMLP_WIDTH]
    v = _layer_norm(z[:, GMLP_WIDTH:], lng_ref[...], lnb_ref[...]).astype(BF16)
    t_idx = jax.lax.broadcasted_iota(jnp.int32, (BLOCK, BLOCK), 0)
    s_idx = jax.lax.broadcasted_iota(jnp.int32, (BLOCK, BLOCK), 1)
    causal = s_idx <= t_idx
    for g in range(GMLP_GROUPS):
        w = jnp.where(causal, ws_ref[g], 0.0).astype(BF16)
        cols = slice(g * GMLP_GROUP_DIM, (g + 1) * GMLP_GROUP_DIM)
        for c in range(n_blk):
            rows = slice(c * BLOCK, (c + 1) * BLOCK)
            mixed = _dot(w, v[rows, cols]) + bs_ref[:, cols]
            sg_ref[rows, cols] = (u[rows, cols] * mixed).astype(BF16)

    y_a = _dot(attn_ref[...], w_bra_ref[...])
    g_a = jax.nn.sigmoid(_dot(xb, w_in_ref[:, OFF_GA:OFF_GG]) + b_in_ref[:, OFF_GA:OFF_GG])
    mix = g_a * y_a
    y_g = _dot(sg_ref[...], w_brg_ref[...])
    g_g = jax.nn.sigmoid(_dot(xb, w_in_ref[:, OFF_GG:IN_WIDTH]) + b_in_ref[:, OFF_GG:IN_WIDTH])
    mix = mix + g_g * y_g
    out = _dot(mix.astype(BF16), w_out_ref[...])
    o_ref[...] = _layer_norm(DEEPNORM_ALPHA * x + out, ln1g_ref[...], ln1b_ref[...])


def _mlp_kernel(x_ref, w_up_ref, w_down_ref, g_ref, b_ref, o_ref):
    x = x_ref[...]
    xb = x.astype(BF16)
    acc = None
    for c in range(D_FF // FF_CHUNK):
        cols = slice(c * FF_CHUNK, (c + 1) * FF_CHUNK)
        h = jnp.maximum(_dot(xb, w_up_ref[:, cols]), 0.0)
        part = _dot((h * h).astype(BF16), w_down_ref[cols, :])
        acc = part if acc is None else acc + part
    o_ref[...] = _layer_norm(DEEPNORM_ALPHA * x + acc, g_ref[...], b_ref[...])


def _resident(shape):
    return pl.BlockSpec(shape, lambda *_: (0,) * len(shape), pipeline_mode=pl.Buffered(1))


def _mixer_call(x, w_in, b_in, sinks, lng, lnb, w_s, bs_tile, w_bra, w_brg, w_out, ln1g, ln1b):
    b, s, d = x.shape
    assert d == D_MODEL and s % TM_MIX == 0 and TM_MIX % BLOCK == 0
    grid = (b, s // TM_MIX)
    tile = pl.BlockSpec((None, TM_MIX, D_MODEL), lambda i, j: (i, j, 0))
    return pl.pallas_call(
        _mixer_kernel,
        out_shape=jax.ShapeDtypeStruct(x.shape, F32),
        grid=grid,
        in_specs=[
            pl.BlockSpec(memory_space=pltpu.SMEM),
            tile,
            _resident((D_MODEL, IN_WIDTH)),
            _resident((1, IN_WIDTH)),
            _resident((1, GMLP_WIDTH)),
            _resident((1, GMLP_WIDTH)),
            _resident((GMLP_GROUPS, BLOCK, BLOCK)),
            _resident((BLOCK, GMLP_WIDTH)),
            _resident((Q_WIDTH, D_MODEL)),
            _resident((GMLP_WIDTH, D_MODEL)),
            _resident((D_MODEL, D_MODEL)),
            _resident((1, D_MODEL)),
            _resident((1, D_MODEL)),
        ],
        out_specs=tile,
        scratch_shapes=[
            pltpu.VMEM((2, BLOCK, KV_WIDTH), BF16),
            pltpu.VMEM((2, BLOCK, KV_WIDTH), BF16),
            pltpu.VMEM((TM_MIX, Q_WIDTH), BF16),
            pltpu.VMEM((TM_MIX, GMLP_WIDTH), BF16),
        ],
        compiler_params=pltpu.CompilerParams(
            dimension_semantics=("arbitrary", "arbitrary"),
            vmem_limit_bytes=V7X_VMEM_LIMIT_BYTES),
        name="token_mixer",
    )(sinks, x, w_in, b_in, lng, lnb, w_s, bs_tile, w_bra, w_brg, w_out, ln1g, ln1b)


def _mlp_call(x2d, w_up, w_down, g, b):
    t, d = x2d.shape
    assert d == D_MODEL and t % TM_MLP == 0
    tile = pl.BlockSpec((TM_MLP, D_MODEL), lambda i: (i, 0))
    return pl.pallas_call(
        _mlp_kernel,
        out_shape=jax.ShapeDtypeStruct(x2d.shape, F32),
        grid=(t // TM_MLP,),
        in_specs=[tile, _resident((D_MODEL, D_FF)), _resident((D_FF, D_MODEL)),
                  _resident((1, D_MODEL)), _resident((1, D_MODEL))],
        out_specs=tile,
        compiler_params=pltpu.CompilerParams(
            dimension_semantics=("arbitrary",),
            vmem_limit_bytes=V7X_VMEM_LIMIT_BYTES),
        name="channel_mixer",
    )(x2d, w_up, w_down, g, b)


def kernel(x, w_in, b_in, attn_sinks, gmlp_ln_g, gmlp_ln_b, gmlp_w_s, gmlp_b_s, w_branch_attn,
           w_branch_gmlp, w_out, ln1_g, ln1_b, w_up, w_down, ln2_g, ln2_b):
    assert w_in.shape[0] == DEPTH
    bsz, seq, d = x.shape
    for l in range(DEPTH):
        bs_tile = jnp.repeat(gmlp_b_s[l].T, GMLP_GROUP_DIM, axis=1)
        x = _mixer_call(
            x, w_in[l].astype(BF16), b_in[l][None, :], attn_sinks[l],
            gmlp_ln_g[l][None, :], gmlp_ln_b[l][None, :], gmlp_w_s[l], bs_tile,
            w_branch_attn[l].astype(BF16), w_branch_gmlp[l].astype(BF16), w_out[l].astype(BF16),
            ln1_g[l][None, :], ln1_b[l][None, :])
        x = _mlp_call(x.reshape(bsz * seq, d), w_up[l].astype(BF16), w_down[l].astype(BF16),
                      ln2_g[l][None, :], ln2_b[l][None, :]).reshape(bsz, seq, d)
    return x
```

```python
import jax
import jax.numpy as jnp
from jax.experimental import pallas as pl
from jax.experimental.pallas import tpu as pltpu

D_MODEL = 1024
N_Q_HEADS = 16
N_KV_HEADS = 4
GROUP = N_Q_HEADS // N_KV_HEADS
HEAD_DIM = 64
Q_WIDTH = N_Q_HEADS * HEAD_DIM
KV_WIDTH = N_KV_HEADS * HEAD_DIM
WINDOW = 128
BLOCK = 128
GMLP_WIDTH = 1024
GMLP_GROUPS = 8
GMLP_GROUP_DIM = GMLP_WIDTH // GMLP_GROUPS
D_FF = 4 * D_MODEL
LN_EPS = 1e-5
DEPTH = 1
DEEPNORM_ALPHA = (2 * DEPTH) ** 0.25

OFF_Q = 0
OFF_K = OFF_Q + Q_WIDTH
OFF_V = OFF_K + KV_WIDTH
OFF_Z = OFF_V + KV_WIDTH
OFF_GA = OFF_Z + 2 * GMLP_WIDTH
OFF_GG = OFF_GA + D_MODEL
IN_WIDTH = OFF_GG + D_MODEL

TM_MIX = 256
TM_MLP = 512
FF_CHUNK = 1024
Z_CHUNK = 512
V7X_VMEM_LIMIT_BYTES = 56 * 1024 * 1024

BF16 = jnp.bfloat16
F32 = jnp.float32


def _dot(a, b):
    return jnp.dot(a, b, preferred_element_type=F32)


def _layer_norm(x, g, b):
    mu = jnp.mean(x, axis=-1, keepdims=True)
    xc = x - mu
    var = jnp.mean(xc * xc, axis=-1, keepdims=True)
    return xc * jax.lax.rsqrt(var + LN_EPS) * g + b


def _mixer_kernel(sinks_ref, x_ref, w_in_ref, b_in_ref, lng_ref, lnb_ref, ws_ref, bs_ref,
                  w_bra_ref, w_brg_ref, w_out_ref, ln1g_ref, ln1b_ref,
                  o_ref, kprev_ref, vprev_ref, attn_ref, sg_ref):
    s = pl.program_id(1)
    n_blk = TM_MIX // BLOCK

    x = x_ref[...]
    xb = x.astype(BF16)

    qkv = _dot(xb, w_in_ref[:, OFF_Q:OFF_Z]) + b_in_ref[:, OFF_Q:OFF_Z]
    q = (qkv[:, OFF_Q:OFF_K] * (HEAD_DIM ** -0.5)).astype(BF16)

    k_new = qkv[:, OFF_K:OFF_V].astype(BF16)
    v_new = qkv[:, OFF_V:OFF_Z].astype(BF16)
    no_carry = jnp.zeros((BLOCK, KV_WIDTH), BF16)
    k_prev = jnp.where(s > 0, kprev_ref[s % 2], no_carry)
    v_prev = jnp.where(s > 0, vprev_ref[s % 2], no_carry)
    k_all = jnp.concatenate([k_prev, k_new], axis=0)
    v_all = jnp.concatenate([v_prev, v_new], axis=0)

    zs = []
    for c in range(2 * GMLP_WIDTH // Z_CHUNK):
        cols = slice(OFF_Z + c * Z_CHUNK, OFF_Z + (c + 1) * Z_CHUNK)
        zs.append(jax.nn.gelu(_dot(xb, w_in_ref[:, cols]) + b_in_ref[:, cols], approximate=True))
    half = GMLP_WIDTH // Z_CHUNK
    u = jnp.concatenate(zs[:half], axis=1)
    vz = jnp.concatenate(zs[half:], axis=1)

    q_pos = jax.lax.broadcasted_iota(jnp.int32, (BLOCK, 2 * BLOCK), 0) + BLOCK
    k_pos = jax.lax.broadcasted_iota(jnp.int32, (BLOCK, 2 * BLOCK), 1)
    diff = q_pos - k_pos
    band = (diff >= 0) & (diff < WINDOW)

    for j in range(n_blk):
        first_key = jnp.where(s * n_blk + j > 0, 0, BLOCK)
        mask = band & (k_pos >= first_key)
        kb = k_all[j * BLOCK:(j + 2) * BLOCK, :]
        vb = v_all[j * BLOCK:(j + 2) * BLOCK, :]
        for hq in range(N_Q_HEADS):
            hk = hq // GROUP
            qh = q[j * BLOCK:(j + 1) * BLOCK, hq * HEAD_DIM:(hq + 1) * HEAD_DIM]
            kh = kb[:, hk * HEAD_DIM:(hk + 1) * HEAD_DIM]
            vh = vb[:, hk * HEAD_DIM:(hk + 1) * HEAD_DIM]
            sc = jax.lax.dot_general(qh, kh, (((1,), (1,)), ((), ())),
                                     preferred_element_type=F32)
            sc = jnp.where(mask, sc, -jnp.inf)
            sink = sinks_ref[hq]
            m = jnp.maximum(jnp.max(sc, axis=-1, keepdims=True), sink)
            p = jnp.exp(sc - m)
            denom = jnp.sum(p, axis=-1, keepdims=True) + jnp.exp(sink - m)
            oh = _dot(p.astype(BF16), vh) / denom
            attn_ref[j * BLOCK:(j + 1) * BLOCK, hq * HEAD_DIM:(hq + 1) * HEAD_DIM] = oh.astype(BF16)

    kprev_ref[(s + 1) % 2] = k_new[TM_MIX - BLOCK:, :]
    vprev_ref[(s + 1) % 2] = v_new[TM_MIX - BLOCK:, :]

    pre_a = _dot(xb, w_in_ref[:, OFF_GA:OFF_GG]) + b_in_ref[:, OFF_GA:OFF_GG]
    v = _layer_norm(vz, lng_ref[...], lnb_ref[...]).astype(BF16)
    g_a = jax.nn.sigmoid(pre_a)

    t_idx = jax.lax.broadcasted_iota(jnp.int32, (BLOCK, BLOCK), 0)
    s_idx = jax.lax.broadcasted_iota(jnp.int32, (BLOCK, BLOCK), 1)
    causal = s_idx <= t_idx
    for g in range(GMLP_GROUPS):
        w = jnp.where(causal, ws_ref[g], 0.0).astype(BF16)
        cols = slice(g * GMLP_GROUP_DIM, (g + 1) * GMLP_GROUP_DIM)
        for c in range(n_blk):
            rows = slice(c * BLOCK, (c + 1) * BLOCK)
            mixed = _dot(w, v[rows, cols]) + bs_ref[:, cols]
            sg_ref[rows, cols] = (u[rows, cols] * mixed).astype(BF16)

    pre_g = _dot(xb, w_in_ref[:, OFF_GG:IN_WIDTH]) + b_in_ref[:, OFF_GG:IN_WIDTH]
    y_a = _dot(attn_ref[...], w_bra_ref[...])
    g_g = jax.nn.sigmoid(pre_g)
    mix = g_a * y_a
    y_g = _dot(sg_ref[...], w_brg_ref[...])
    mix = (mix + g_g * y_g).astype(BF16)
    for c in range(n_blk):
        rows = slice(c * BLOCK, (c + 1) * BLOCK)
        out = _dot(mix[rows, :], w_out_ref[...])
        o_ref[rows, :] = _layer_norm(DEEPNORM_ALPHA * x[rows, :] + out, ln1g_ref[...], ln1b_ref[...])


def _mlp_kernel(x_ref, w_up_ref, w_down_ref, g_ref, b_ref, o_ref):
    x = x_ref[...]
    xb = x.astype(BF16)
    n_chunks = D_FF // FF_CHUNK
    acc = None
    for c in range(n_chunks):
        cols = slice(c * FF_CHUNK, (c + 1) * FF_CHUNK)
        h = jnp.maximum(_dot(xb, w_up_ref[:, cols]), 0.0)
        h = (h * h).astype(BF16)
        if c < n_chunks - 1:
            part = _dot(h, w_down_ref[cols, :])
            acc = part if acc is None else acc + part
    half = TM_MLP // 2
    for r in range(2):
        rows = slice(r * half, (r + 1) * half)
        y = acc[rows, :] + _dot(h[rows, :], w_down_ref[cols, :])
        o_ref[rows, :] = _layer_norm(DEEPNORM_ALPHA * x[rows, :] + y, g_ref[...], b_ref[...])


def _resident(shape):
    return pl.BlockSpec(shape, lambda *_: (0,) * len(shape), pipeline_mode=pl.Buffered(1))


def _mixer_call(x, w_in, b_in, sinks, lng, lnb, w_s, bs_tile, w_bra, w_brg, w_out, ln1g, ln1b):
    b, s, d = x.shape
    assert d == D_MODEL and s % TM_MIX == 0 and TM_MIX % BLOCK == 0
    grid = (b, s // TM_MIX)
    tile = pl.BlockSpec((None, TM_MIX, D_MODEL), lambda i, j: (i, j, 0))
    return pl.pallas_call(
        _mixer_kernel,
        out_shape=jax.ShapeDtypeStruct(x.shape, F32),
        grid=grid,
        in_specs=[
            pl.BlockSpec(memory_space=pltpu.SMEM),
            tile,
            _resident((D_MODEL, IN_WIDTH)),
            _resident((1, IN_WIDTH)),
            _resident((1, GMLP_WIDTH)),
            _resident((1, GMLP_WIDTH)),
            _resident((GMLP_GROUPS, BLOCK, BLOCK)),
            _resident((BLOCK, GMLP_WIDTH)),
            _resident((Q_WIDTH, D_MODEL)),
            _resident((GMLP_WIDTH, D_MODEL)),
            _resident((D_MODEL, D_MODEL)),
            _resident((1, D_MODEL)),
            _resident((1, D_MODEL)),
        ],
        out_specs=tile,
        scratch_shapes=[
            pltpu.VMEM((2, BLOCK, KV_WIDTH), BF16),
            pltpu.VMEM((2, BLOCK, KV_WIDTH), BF16),
            pltpu.VMEM((TM_MIX, Q_WIDTH), BF16),
            pltpu.VMEM((TM_MIX, GMLP_WIDTH), BF16),
        ],
        compiler_params=pltpu.CompilerParams(
            dimension_semantics=("arbitrary", "arbitrary"),
            vmem_limit_bytes=V7X_VMEM_LIMIT_BYTES),
        name="token_mixer",
    )(sinks, x, w_in, b_in, lng, lnb, w_s, bs_tile, w_bra, w_brg, w_out, ln1g, ln1b)


def _mlp_call(x2d, w_up, w_down, g, b):
    t, d = x2d.shape
    assert d == D_MODEL and t % TM_MLP == 0
    tile = pl.BlockSpec((TM_MLP, D_MODEL), lambda i: (i, 0))
    return pl.pallas_call(
        _mlp_kernel,
        out_shape=jax.ShapeDtypeStruct(x2d.shape, F32),
        grid=(t // TM_MLP,),
        in_specs=[tile, _resident((D_MODEL, D_FF)), _resident((D_FF, D_MODEL)),
                  _resident((1, D_MODEL)), _resident((1, D_MODEL))],
        out_specs=tile,
        compiler_params=pltpu.CompilerParams(
            dimension_semantics=("arbitrary",),
            vmem_limit_bytes=V7X_VMEM_LIMIT_BYTES),
        name="channel_mixer",
    )(x2d, w_up, w_down, g, b)


def kernel(x, w_in, b_in, attn_sinks, gmlp_ln_g, gmlp_ln_b, gmlp_w_s, gmlp_b_s, w_branch_attn,
           w_branch_gmlp, w_out, ln1_g, ln1_b, w_up, w_down, ln2_g, ln2_b):
    assert w_in.shape[0] == DEPTH
    bsz, seq, d = x.shape
    for l in range(DEPTH):
        bs_tile = jnp.repeat(gmlp_b_s[l].T, GMLP_GROUP_DIM, axis=1)
        x = _mixer_call(
            x, w_in[l].astype(BF16), b_in[l][None, :], attn_sinks[l],
            gmlp_ln_g[l][None, :], gmlp_ln_b[l][None, :], gmlp_w_s[l], bs_tile,
            w_branch_attn[l].astype(BF16), w_branch_gmlp[l].astype(BF16), w_out[l].astype(BF16),
            ln1_g[l][None, :], ln1_b[l][None, :])
        x = _mlp_call(x.reshape(bsz * seq, d), w_up[l].astype(BF16), w_down[l].astype(BF16),
                      ln2_g[l][None, :], ln2_b[l][None, :]).reshape(bsz, seq, d)
    return x
```

```python
import jax
import jax.numpy as jnp
from jax.experimental import pallas as pl
from jax.experimental.pallas import tpu as pltpu

D_MODEL = 1024
N_Q_HEADS = 16
N_KV_HEADS = 4
GROUP = N_Q_HEADS // N_KV_HEADS
HEAD_DIM = 64
Q_WIDTH = N_Q_HEADS * HEAD_DIM
KV_WIDTH = N_KV_HEADS * HEAD_DIM
WINDOW = 128
BLOCK = 128
GMLP_WIDTH = 1024
GMLP_GROUPS = 8
GMLP_GROUP_DIM = GMLP_WIDTH // GMLP_GROUPS
D_FF = 4 * D_MODEL
LN_EPS = 1e-5
DEPTH = 1
DEEPNORM_ALPHA = (2 * DEPTH) ** 0.25

OFF_Q = 0
OFF_K = OFF_Q + Q_WIDTH
OFF_V = OFF_K + KV_WIDTH
OFF_Z = OFF_V + KV_WIDTH
OFF_GA = OFF_Z + 2 * GMLP_WIDTH
OFF_GG = OFF_GA + D_MODEL
IN_WIDTH = OFF_GG + D_MODEL

TM_SUB = 256
N_SUB = 2
SUB_LAG = 3
TM_MIX = N_SUB * TM_SUB
TM_MLP = 512
FF_CHUNK = 1024
P_CHUNK = 512
V7X_VMEM_LIMIT_BYTES = 56 * 1024 * 1024

BF16 = jnp.bfloat16
F32 = jnp.float32


def _dot(a, b):
    return jnp.dot(a, b, preferred_element_type=F32)


def _layer_norm(x, g, b):
    mu = jnp.mean(x, axis=-1, keepdims=True)
    xc = x - mu
    var = jnp.mean(xc * xc, axis=-1, keepdims=True)
    return xc * jax.lax.rsqrt(var + LN_EPS) * g + b


def _mixer_kernel(sinks_ref, x_ref, w_qvt_ref, b_qt_ref, b_vt_ref, w_in_ref, b_in_ref,
                  lng_ref, lnb_ref, ws_ref, bs_ref,
                  w_bra_ref, w_brg_ref, w_out_ref, ln1g_ref, ln1b_ref,
                  o_ref, kprev_ref, vprev_ref, attn_ref, sg_ref):
    s = pl.program_id(1)
    n_blk = TM_SUB // BLOCK

    k_pos = jax.lax.broadcasted_iota(jnp.int32, (2 * BLOCK, BLOCK), 0)
    q_pos = jax.lax.broadcasted_iota(jnp.int32, (2 * BLOCK, BLOCK), 1)
    diff = k_pos - q_pos
    band = (diff >= 1) & (diff <= WINDOW)
    pair_width = 2 * HEAD_DIM
    zero_rows = jnp.zeros((HEAD_DIM, GROUP * BLOCK), BF16)

    t_idx = jax.lax.broadcasted_iota(jnp.int32, (BLOCK, BLOCK), 0)
    s_idx = jax.lax.broadcasted_iota(jnp.int32, (BLOCK, BLOCK), 1)
    causal = s_idx <= t_idx

    carry = {
        "k": jnp.where(s > 0, kprev_ref[s % 2], jnp.zeros((BLOCK, KV_WIDTH), BF16)),
        "v_t": jnp.where(s > 0, vprev_ref[s % 2], jnp.zeros((KV_WIDTH, BLOCK), BF16)),
    }

    def subtile(r):
        rows_r = slice(r * TM_SUB, (r + 1) * TM_SUB)
        x = x_ref[rows_r, :]
        xb = x.astype(BF16)

        def proj(lo, hi):
            return _dot(xb, w_in_ref[:, lo:hi]) + b_in_ref[:, lo:hi]

        def gelu_chunk(c):
            return jax.nn.gelu(proj(OFF_Z + c * P_CHUNK, OFF_Z + (c + 1) * P_CHUNK),
                               approximate=True)

        qv_t = jax.lax.dot_general(w_qvt_ref[...], xb, (((1,), (1,)), ((), ())),
                                   preferred_element_type=F32)
        q_t = ((qv_t[:Q_WIDTH, :] + b_qt_ref[...]) * (HEAD_DIM ** -0.5)).astype(BF16)
        v_t_new = (qv_t[Q_WIDTH:, :] + b_vt_ref[...]).astype(BF16)
        yield

        k_new = proj(OFF_K, OFF_V).astype(BF16)
        k_all = jnp.concatenate([carry["k"], k_new], axis=0)
        v_t_all = jnp.concatenate([carry["v_t"], v_t_new], axis=1)
        carry["k"] = k_new[TM_SUB - BLOCK:, :]
        carry["v_t"] = v_t_new[:, TM_SUB - BLOCK:]
        if r == N_SUB - 1:
            kprev_ref[(s + 1) % 2] = carry["k"]
            vprev_ref[(s + 1) % 2] = carry["v_t"]
        n_z = 2 * GMLP_WIDTH // P_CHUNK
        zs = [gelu_chunk(0)]
        yield
        zs.append(gelu_chunk(1))
        yield

        def scores_t(j, hk):
            pair, odd = divmod(hk, 2)
            toks = slice(j * BLOCK, (j + 1) * BLOCK)
            k_pair = k_all[j * BLOCK:(j + 2) * BLOCK, pair * pair_width:(pair + 1) * pair_width]
            q_cat = jnp.concatenate(
                [q_t[(hk * GROUP + g) * HEAD_DIM:(hk * GROUP + g + 1) * HEAD_DIM, toks]
                 for g in range(GROUP)], axis=1)
            rhs = jnp.concatenate([zero_rows, q_cat] if odd else [q_cat, zero_rows], axis=0)
            return _dot(k_pair, rhs)

        def softmax_pv(j, hk, sc_t):
            first_key = jnp.where((s * N_SUB + r) * n_blk + j > 0, 0, BLOCK)
            mask = band & (k_pos >= first_key)
            toks = slice(j * BLOCK, (j + 1) * BLOCK)
            p_parts, inv_parts = [], []
            for g in range(GROUP):
                sc = jnp.where(mask, sc_t[:, g * BLOCK:(g + 1) * BLOCK], -jnp.inf)
                sink = sinks_ref[hk * GROUP + g]
                m = jnp.maximum(jnp.max(sc, axis=0, keepdims=True), sink)
                p = jnp.exp(sc - m)
                denom = jnp.sum(p, axis=0, keepdims=True) + jnp.exp(sink - m)
                p_parts.append(p.astype(BF16))
                inv_parts.append(1.0 / denom)
            p_t = jnp.concatenate(p_parts, axis=1)
            v_t = v_t_all[hk * HEAD_DIM:(hk + 1) * HEAD_DIM, j * BLOCK:(j + 2) * BLOCK]
            o_t = _dot(v_t, p_t)
            for g in range(GROUP):
                feat = slice((hk * GROUP + g) * HEAD_DIM, (hk * GROUP + g + 1) * HEAD_DIM)
                attn_ref[r, feat, toks] = (o_t[:, g * BLOCK:(g + 1) * BLOCK]
                                           * inv_parts[g]).astype(BF16)

        side_work = [lambda c=c: gelu_chunk(c) for c in range(2, n_z)]
        side_work += [lambda c=c: proj(OFF_GA + c * P_CHUNK, OFF_GA + (c + 1) * P_CHUNK)
                      for c in range(D_MODEL // P_CHUNK)]
        items = [(j, hk) for j in range(n_blk) for hk in range(N_KV_HEADS)]
        every = len(items) // len(side_work)
        side_out = []
        sc_next = scores_t(*items[0])
        for i, item in enumerate(items):
            sc_cur = sc_next
            if i + 1 < len(items):
                sc_next = scores_t(*items[i + 1])
            softmax_pv(*item, sc_cur)
            if i % every == every - 1 and i // every < len(side_work):
                side_out.append(side_work[i // every]())
            yield
        zs += side_out[:n_z - 2]
        u = jnp.concatenate(zs[:n_z // 2], axis=1)
        vz = jnp.concatenate(zs[n_z // 2:], axis=1)
        pre_a = jnp.concatenate(side_out[n_z - 2:], axis=1)

        pre_g = proj(OFF_GG, IN_WIDTH)
        v = _layer_norm(vz, lng_ref[...], lnb_ref[...]).astype(BF16)
        g_a = jax.nn.sigmoid(pre_a)
        yield

        for g in range(GMLP_GROUPS):
            w = jnp.where(causal, ws_ref[g], 0.0).astype(BF16)
            cols = slice(g * GMLP_GROUP_DIM, (g + 1) * GMLP_GROUP_DIM)
            for c in range(n_blk):
                rows = slice(c * BLOCK, (c + 1) * BLOCK)
                mixed = _dot(w, v[rows, cols]) + bs_ref[:, cols]
                sg_ref[r, rows, cols] = (u[rows, cols] * mixed).astype(BF16)
        g_g = jax.nn.sigmoid(pre_g)
        yield

        y_a = jax.lax.dot_general(attn_ref[r], w_bra_ref[...], (((0,), (0,)), ((), ())),
                                  preferred_element_type=F32)
        mix = g_a * y_a
        yield
        y_g = _dot(sg_ref[r], w_brg_ref[...])
        mix = (mix + g_g * y_g).astype(BF16)
        yield
        for c in range(n_blk):
            rows = slice(c * BLOCK, (c + 1) * BLOCK)
            out = _dot(mix[rows, :], w_out_ref[...])
            o_ref[r * TM_SUB + c * BLOCK:r * TM_SUB + (c + 1) * BLOCK, :] = _layer_norm(
                DEEPNORM_ALPHA * x[rows, :] + out, ln1g_ref[...], ln1b_ref[...])
            yield

    gens = [subtile(r) for r in range(N_SUB)]
    started = [0] * N_SUB
    alive = [True] * N_SUB
    while any(alive):
        for r, gen in enumerate(gens):
            if not alive[r] or (r > 0 and alive[r - 1] and started[r - 1] < SUB_LAG):
                continue
            started[r] += 1
            alive[r] = next(gen, "done") != "done"


def _mlp_kernel(x_ref, w_up_ref, w_down_ref, g_ref, b_ref, o_ref):
    x = x_ref[...]
    xb = x.astype(BF16)
    n_chunks = D_FF // FF_CHUNK
    acc = None
    for c in range(n_chunks):
        cols = slice(c * FF_CHUNK, (c + 1) * FF_CHUNK)
        h = jnp.maximum(_dot(xb, w_up_ref[:, cols]), 0.0)
        h = (h * h).astype(BF16)
        if c < n_chunks - 1:
            part = _dot(h, w_down_ref[cols, :])
            acc = part if acc is None else acc + part
    half = TM_MLP // 2
    for r in range(2):
        rows = slice(r * half, (r + 1) * half)
        y = acc[rows, :] + _dot(h[rows, :], w_down_ref[cols, :])
        o_ref[rows, :] = _layer_norm(DEEPNORM_ALPHA * x[rows, :] + y, g_ref[...], b_ref[...])


def _resident(shape):
    return pl.BlockSpec(shape, lambda *_: (0,) * len(shape), pipeline_mode=pl.Buffered(1))


def _mixer_call(x, w_qvt, b_qt, b_vt, w_in, b_in, sinks, lng, lnb, w_s, bs_tile,
                w_bra, w_brg, w_out, ln1g, ln1b):
    b, s, d = x.shape
    assert d == D_MODEL and s % TM_MIX == 0 and TM_SUB % BLOCK == 0
    grid = (b, s // TM_MIX)
    tile = pl.BlockSpec((None, TM_MIX, D_MODEL), lambda i, j: (i, j, 0))
    return pl.pallas_call(
        _mixer_kernel,
        out_shape=jax.ShapeDtypeStruct(x.shape, F32),
        grid=grid,
        in_specs=[
            pl.BlockSpec(memory_space=pltpu.SMEM),
            tile,
            _resident((Q_WIDTH + KV_WIDTH, D_MODEL)),
            _resident((Q_WIDTH, TM_SUB)),
            _resident((KV_WIDTH, TM_SUB)),
            _resident((D_MODEL, IN_WIDTH)),
            _resident((1, IN_WIDTH)),
            _resident((1, GMLP_WIDTH)),
            _resident((1, GMLP_WIDTH)),
            _resident((GMLP_GROUPS, BLOCK, BLOCK)),
            _resident((BLOCK, GMLP_WIDTH)),
            _resident((Q_WIDTH, D_MODEL)),
            _resident((GMLP_WIDTH, D_MODEL)),
            _resident((D_MODEL, D_MODEL)),
            _resident((1, D_MODEL)),
            _resident((1, D_MODEL)),
        ],
        out_specs=tile,
        scratch_shapes=[
            pltpu.VMEM((2, BLOCK, KV_WIDTH), BF16),
            pltpu.VMEM((2, KV_WIDTH, BLOCK), BF16),
            pltpu.VMEM((N_SUB, Q_WIDTH, TM_SUB), BF16),
            pltpu.VMEM((N_SUB, TM_SUB, GMLP_WIDTH), BF16),
        ],
        compiler_params=pltpu.CompilerParams(
            dimension_semantics=("arbitrary", "arbitrary"),
            vmem_limit_bytes=V7X_VMEM_LIMIT_BYTES),
        name="token_mixer",
    )(sinks, x, w_qvt, b_qt, b_vt, w_in, b_in, lng, lnb, w_s, bs_tile, w_bra, w_brg, w_out,
      ln1g, ln1b)


def _mlp_call(x2d, w_up, w_down, g, b):
    t, d = x2d.shape
    assert d == D_MODEL and t % TM_MLP == 0
    tile = pl.BlockSpec((TM_MLP, D_MODEL), lambda i: (i, 0))
    return pl.pallas_call(
        _mlp_kernel,
        out_shape=jax.ShapeDtypeStruct(x2d.shape, F32),
        grid=(t // TM_MLP,),
        in_specs=[tile, _resident((D_MODEL, D_FF)), _resident((D_FF, D_MODEL)),
                  _resident((1, D_MODEL)), _resident((1, D_MODEL))],
        out_specs=tile,
        compiler_params=pltpu.CompilerParams(
            dimension_semantics=("arbitrary",),
            vmem_limit_bytes=V7X_VMEM_LIMIT_BYTES),
        name="channel_mixer",
    )(x2d, w_up, w_down, g, b)


def kernel(x, w_in, b_in, attn_sinks, gmlp_ln_g, gmlp_ln_b, gmlp_w_s, gmlp_b_s, w_branch_attn,
           w_branch_gmlp, w_out, ln1_g, ln1_b, w_up, w_down, ln2_g, ln2_b):
    assert w_in.shape[0] == DEPTH
    bsz, seq, d = x.shape
    for l in range(DEPTH):
        bs_tile = jnp.repeat(gmlp_b_s[l].T, GMLP_GROUP_DIM, axis=1)
        w_qvt = jnp.concatenate([w_in[l][:, OFF_Q:OFF_K], w_in[l][:, OFF_V:OFF_Z]], axis=1).T
        b_qt = jnp.broadcast_to(b_in[l][OFF_Q:OFF_K, None], (Q_WIDTH, TM_SUB))
        b_vt = jnp.broadcast_to(b_in[l][OFF_V:OFF_Z, None], (KV_WIDTH, TM_SUB))
        x = _mixer_call(
            x, w_qvt.astype(BF16), b_qt, b_vt, w_in[l].astype(BF16), b_in[l][None, :],
            attn_sinks[l], gmlp_ln_g[l][None, :], gmlp_ln_b[l][None, :], gmlp_w_s[l], bs_tile,
            w_branch_attn[l].astype(BF16), w_branch_gmlp[l].astype(BF16), w_out[l].astype(BF16),
            ln1_g[l][None, :], ln1_b[l][None, :])
        x = _mlp_call(x.reshape(bsz * seq, d), w_up[l].astype(BF16), w_down[l].astype(BF16),
                      ln2_g[l][None, :], ln2_b[l][None, :]).reshape(bsz, seq, d)
    return x
```

```python
import functools

import jax
import jax.numpy as jnp
from jax.experimental import pallas as pl
from jax.experimental.pallas import tpu as pltpu

D_MODEL = 1024
N_Q_HEADS = 16
N_KV_HEADS = 4
GROUP = N_Q_HEADS // N_KV_HEADS
HEAD_DIM = 64
Q_WIDTH = N_Q_HEADS * HEAD_DIM
KV_WIDTH = N_KV_HEADS * HEAD_DIM
WINDOW = 128
BLOCK = 128
GMLP_WIDTH = 1024
GMLP_GROUPS = 8
GMLP_GROUP_DIM = GMLP_WIDTH // GMLP_GROUPS
D_FF = 4 * D_MODEL
LN_EPS = 1e-5
DEPTH = 1
DEEPNORM_ALPHA = (2 * DEPTH) ** 0.25

OFF_Q = 0
OFF_K = OFF_Q + Q_WIDTH
OFF_V = OFF_K + KV_WIDTH
OFF_Z = OFF_V + KV_WIDTH
OFF_GA = OFF_Z + 2 * GMLP_WIDTH
OFF_GG = OFF_GA + D_MODEL
IN_WIDTH = OFF_GG + D_MODEL
R_K = 0
R_Z = R_K + KV_WIDTH
R_GA = R_Z + 2 * GMLP_WIDTH
R_GG = R_GA + D_MODEL
R_WIDTH = R_GG + D_MODEL

TM = 256
FF_CHUNK = 512
P_CHUNK = 256
V7X_VMEM_LIMIT_BYTES = 60 * 1024 * 1024

BF16 = jnp.bfloat16
F32 = jnp.float32


def _dot(a, b):
    return jnp.dot(a, b, preferred_element_type=F32)


def _layer_norm(x, g, b):
    mu = jnp.mean(x, axis=-1, keepdims=True)
    xc = x - mu
    var = jnp.mean(xc * xc, axis=-1, keepdims=True)
    return xc * jax.lax.rsqrt(var + LN_EPS) * g + b


def _layer_kernel(tiles_per_seq,
                  sinks_ref, x_ref, w_qvt_ref, b_qt_ref, b_vt_ref, w_r_ref, b_r_ref,
                  lng_ref, lnb_ref, ws_ref, bs_ref, w_bra_ref, w_brg_ref, w_out_ref,
                  ln1g_ref, ln1b_ref, w_up_ref, w_down_ref, ln2g_ref, ln2b_ref,
                  o_ref, kprev_ref, vprev_ref, resid_ref, attn_ref, sg_ref):
    t = pl.program_id(0)
    n_blk = TM // BLOCK
    rd_slot = (t + 1) % 2
    wr_slot = t % 2
    has_prev_tile = (t % tiles_per_seq) > 0

    @pl.when(t == 0)
    def _():
        resid_ref[1] = jnp.zeros((TM, D_MODEL), F32)

    def token_mixer():
        x = x_ref[...]
        xb = x.astype(BF16)

        def proj(lo, hi):
            return _dot(xb, w_r_ref[:, lo:hi]) + b_r_ref[:, lo:hi]

        def gelu_chunk(c):
            return jax.nn.gelu(proj(R_Z + c * P_CHUNK, R_Z + (c + 1) * P_CHUNK), approximate=True)

        qv_t = jax.lax.dot_general(w_qvt_ref[...], xb, (((1,), (1,)), ((), ())),
                                   preferred_element_type=F32)
        q_t = ((qv_t[:Q_WIDTH, :] + b_qt_ref[...]) * (HEAD_DIM ** -0.5)).astype(BF16)
        v_t_new = (qv_t[Q_WIDTH:, :] + b_vt_ref[...]).astype(BF16)
        yield

        k_new = proj(R_K, R_Z).astype(BF16)
        k_prev = jnp.where(has_prev_tile, kprev_ref[rd_slot], jnp.zeros((BLOCK, KV_WIDTH), BF16))
        v_t_prev = jnp.where(has_prev_tile, vprev_ref[rd_slot], jnp.zeros((KV_WIDTH, BLOCK), BF16))
        k_all = jnp.concatenate([k_prev, k_new], axis=0)
        v_t_all = jnp.concatenate([v_t_prev, v_t_new], axis=1)
        kprev_ref[wr_slot] = k_new[TM - BLOCK:, :]
        vprev_ref[wr_slot] = v_t_new[:, TM - BLOCK:]
        n_z = 2 * GMLP_WIDTH // P_CHUNK
        n_lead = n_z // 2
        zs = [gelu_chunk(c) for c in range(n_lead // 2)]
        yield
        zs += [gelu_chunk(c) for c in range(n_lead // 2, n_lead)]
        yield

        k_pos = jax.lax.broadcasted_iota(jnp.int32, (2 * BLOCK, BLOCK), 0)
        q_pos = jax.lax.broadcasted_iota(jnp.int32, (2 * BLOCK, BLOCK), 1)
        diff = k_pos - q_pos
        band = (diff >= 1) & (diff <= WINDOW)
        pair_width = 2 * HEAD_DIM
        zero_rows = jnp.zeros((HEAD_DIM, GROUP * BLOCK), BF16)

        def scores_t(j, hk):
            pair, odd = divmod(hk, 2)
            toks = slice(j * BLOCK, (j + 1) * BLOCK)
            k_pair = k_all[j * BLOCK:(j + 2) * BLOCK, pair * pair_width:(pair + 1) * pair_width]
            q_cat = jnp.concatenate(
                [q_t[(hk * GROUP + g) * HEAD_DIM:(hk * GROUP + g + 1) * HEAD_DIM, toks]
                 for g in range(GROUP)], axis=1)
            rhs = jnp.concatenate([zero_rows, q_cat] if odd else [q_cat, zero_rows], axis=0)
            return _dot(k_pair, rhs)

        def softmax_pv(j, hk, sc_t):
            first_key = jnp.where(has_prev_tile, 0, BLOCK) if j == 0 else 0
            mask = band & (k_pos >= first_key)
            toks = slice(j * BLOCK, (j + 1) * BLOCK)
            p_parts, inv_parts = [], []
            for g in range(GROUP):
                sc = jnp.where(mask, sc_t[:, g * BLOCK:(g + 1) * BLOCK], -jnp.inf)
                sink = sinks_ref[hk * GROUP + g]
                m = jnp.maximum(jnp.max(sc, axis=0, keepdims=True), sink)
                p = jnp.exp(sc - m)
                denom = jnp.sum(p, axis=0, keepdims=True) + jnp.exp(sink - m)
                p_parts.append(p.astype(BF16))
                inv_parts.append(1.0 / denom)
            p_t = jnp.concatenate(p_parts, axis=1)
            v_t = v_t_all[hk * HEAD_DIM:(hk + 1) * HEAD_DIM, j * BLOCK:(j + 2) * BLOCK]
            o_t = _dot(v_t, p_t)
            for g in range(GROUP):
                feat = slice((hk * GROUP + g) * HEAD_DIM, (hk * GROUP + g + 1) * HEAD_DIM)
                attn_ref[feat, toks] = (o_t[:, g * BLOCK:(g + 1) * BLOCK]
                                        * inv_parts[g]).astype(BF16)

        side_work = [lambda c=c: gelu_chunk(c) for c in range(n_lead, n_z)]
        side_work += [lambda c=c: proj(R_GA + c * P_CHUNK, R_GA + (c + 1) * P_CHUNK)
                      for c in range(D_MODEL // P_CHUNK)]
        items = [(j, hk) for j in range(n_blk) for hk in range(N_KV_HEADS)]
        assert len(side_work) == len(items)
        side_out = []
        sc_next = scores_t(*items[0])
        for i, item in enumerate(items):
            sc_cur = sc_next
            if i + 1 < len(items):
                sc_next = scores_t(*items[i + 1])
            softmax_pv(*item, sc_cur)
            side_out.append(side_work[i]())
            yield
        zs += side_out[:n_z - n_lead]
        u = jnp.concatenate(zs[:n_z // 2], axis=1)
        vz = jnp.concatenate(zs[n_z // 2:], axis=1)
        pre_a = jnp.concatenate(side_out[n_z - n_lead:], axis=1)

        pre_g = proj(R_GG, R_WIDTH)
        v = _layer_norm(vz, lng_ref[...], lnb_ref[...]).astype(BF16)
        g_a = jax.nn.sigmoid(pre_a)
        yield

        t_idx = jax.lax.broadcasted_iota(jnp.int32, (BLOCK, BLOCK), 0)
        s_idx = jax.lax.broadcasted_iota(jnp.int32, (BLOCK, BLOCK), 1)
        causal = s_idx <= t_idx
        for g in range(GMLP_GROUPS):
            w = jnp.where(causal, ws_ref[g], 0.0).astype(BF16)
            cols = slice(g * GMLP_GROUP_DIM, (g + 1) * GMLP_GROUP_DIM)
            for c in range(n_blk):
                rows = slice(c * BLOCK, (c + 1) * BLOCK)
                mixed = _dot(w, v[rows, cols]) + bs_ref[:, cols]
                sg_ref[rows, cols] = (u[rows, cols] * mixed).astype(BF16)
        g_g = jax.nn.sigmoid(pre_g)
        yield

        y_a = jax.lax.dot_general(attn_ref[...], w_bra_ref[...], (((0,), (0,)), ((), ())),
                                  preferred_element_type=F32)
        mix = g_a * y_a
        yield
        y_g = _dot(sg_ref[...], w_brg_ref[...])
        mix = (mix + g_g * y_g).astype(BF16)
        yield
        resid_ref[wr_slot] = DEEPNORM_ALPHA * x + _dot(mix, w_out_ref[...])
        yield

    def channel_mixer():
        x1 = _layer_norm(resid_ref[rd_slot], ln1g_ref[...], ln1b_ref[...])
        x1b = x1.astype(BF16)
        yield
        n_chunks = D_FF // FF_CHUNK
        acc = None
        for c in range(n_chunks):
            cols = slice(c * FF_CHUNK, (c + 1) * FF_CHUNK)
            h = jnp.maximum(_dot(x1b, w_up_ref[:, cols]), 0.0)
            h = (h * h).astype(BF16)
            yield
            if c < n_chunks - 1:
                part = _dot(h, w_down_ref[cols, :])
                acc = part if acc is None else acc + part
                yield
        for r in range(n_blk):
            rows = slice(r * BLOCK, (r + 1) * BLOCK)
            y = acc[rows, :] + _dot(h[rows, :], w_down_ref[cols, :])
            o_ref[rows, :] = _layer_norm(DEEPNORM_ALPHA * x1[rows, :] + y,
                                         ln2g_ref[...], ln2b_ref[...])
            yield

    tm_units = token_mixer()
    cm_units = channel_mixer()
    after = {}
    for i, u_idx in enumerate(CM_AFTER):
        after.setdefault(u_idx, []).append(i)
    n_tm = 0
    while next(tm_units, "done") != "done":
        for _ in after.pop(n_tm, []):
            next(cm_units)
        n_tm += 1
    leftover = next(cm_units, "done")
    assert not after and leftover == "done", (after, n_tm)


CM_AFTER = (0, 1, 2, 3, 4, 5, 6, 7, 8, 9, 10, 11, 11, 12, 12, 13, 13, 14)


def _resident(shape):
    return pl.BlockSpec(shape, lambda *_: (0,) * len(shape), pipeline_mode=pl.Buffered(1))


def _layer_call(x2d, tiles_per_seq, sinks, w_qvt, b_qt, b_vt, w_r, b_r, lng, lnb, w_s, bs_tile,
                w_bra, w_brg, w_out, ln1g, ln1b, w_up, w_down, ln2g, ln2b):
    n_tok, d = x2d.shape
    assert d == D_MODEL and n_tok % TM == 0 and TM % BLOCK == 0
    n_tiles = n_tok // TM
    return pl.pallas_call(
        functools.partial(_layer_kernel, tiles_per_seq),
        out_shape=jax.ShapeDtypeStruct(x2d.shape, F32),
        grid=(n_tiles + 1,),
        in_specs=[
            pl.BlockSpec(memory_space=pltpu.SMEM),
            pl.BlockSpec((TM, D_MODEL), lambda t: (jnp.minimum(t, n_tiles - 1), 0)),
            _resident((Q_WIDTH + KV_WIDTH, D_MODEL)),
            _resident((Q_WIDTH, TM)),
            _resident((KV_WIDTH, TM)),
            _resident((D_MODEL, R_WIDTH)),
            _resident((1, R_WIDTH)),
            _resident((1, GMLP_WIDTH)),
            _resident((1, GMLP_WIDTH)),
            _resident((GMLP_GROUPS, BLOCK, BLOCK)),
            _resident((BLOCK, GMLP_WIDTH)),
            _resident((Q_WIDTH, D_MODEL)),
            _resident((GMLP_WIDTH, D_MODEL)),
            _resident((D_MODEL, D_MODEL)),
            _resident((1, D_MODEL)),
            _resident((1, D_MODEL)),
            _resident((D_MODEL, D_FF)),
            _resident((D_FF, D_MODEL)),
            _resident((1, D_MODEL)),
            _resident((1, D_MODEL)),
        ],
        out_specs=pl.BlockSpec((TM, D_MODEL), lambda t: (jnp.maximum(t - 1, 0), 0)),
        scratch_shapes=[
            pltpu.VMEM((2, BLOCK, KV_WIDTH), BF16),
            pltpu.VMEM((2, KV_WIDTH, BLOCK), BF16),
            pltpu.VMEM((2, TM, D_MODEL), F32),
            pltpu.VMEM((Q_WIDTH, TM), BF16),
            pltpu.VMEM((TM, GMLP_WIDTH), BF16),
        ],
        compiler_params=pltpu.CompilerParams(
            dimension_semantics=("arbitrary",),
            vmem_limit_bytes=V7X_VMEM_LIMIT_BYTES),
        name="decoder_layer",
    )(sinks, x2d, w_qvt, b_qt, b_vt, w_r, b_r, lng, lnb, w_s, bs_tile, w_bra, w_brg, w_out,
      ln1g, ln1b, w_up, w_down, ln2g, ln2b)


def kernel(x, w_in, b_in, attn_sinks, gmlp_ln_g, gmlp_ln_b, gmlp_w_s, gmlp_b_s, w_branch_attn,
           w_branch_gmlp, w_out, ln1_g, ln1_b, w_up, w_down, ln2_g, ln2_b):
    assert w_in.shape[0] == DEPTH
    bsz, seq, d = x.shape
    assert seq % TM == 0
    x2d = x.reshape(bsz * seq, d)
    for l in range(DEPTH):
        bs_tile = jnp.repeat(gmlp_b_s[l].T, GMLP_GROUP_DIM, axis=1)
        w_qvt = jnp.concatenate([w_in[l][:, OFF_Q:OFF_K], w_in[l][:, OFF_V:OFF_Z]], axis=1).T
        b_qt = jnp.broadcast_to(b_in[l][OFF_Q:OFF_K, None], (Q_WIDTH, TM))
        b_vt = jnp.broadcast_to(b_in[l][OFF_V:OFF_Z, None], (KV_WIDTH, TM))
        w_r = jnp.concatenate([w_in[l][:, OFF_K:OFF_V], w_in[l][:, OFF_Z:]], axis=1)
        b_r = jnp.concatenate([b_in[l][OFF_K:OFF_V], b_in[l][OFF_Z:]])[None, :]
        x2d = _layer_call(
            x2d, seq // TM, attn_sinks[l], w_qvt.astype(BF16), b_qt, b_vt, w_r.astype(BF16), b_r,
            gmlp_ln_g[l][None, :], gmlp_ln_b[l][None, :], gmlp_w_s[l], bs_tile,
            w_branch_attn[l].astype(BF16), w_branch_gmlp[l].astype(BF16), w_out[l].astype(BF16),
            ln1_g[l][None, :], ln1_b[l][None, :], w_up[l].astype(BF16), w_down[l].astype(BF16),
            ln2_g[l][None, :], ln2_b[l][None, :])
    return x2d.reshape(bsz, seq, d)
```

```python
import functools

import jax
import jax.numpy as jnp
from jax.experimental import pallas as pl
from jax.experimental.pallas import tpu as pltpu

D_MODEL = 1024
N_Q_HEADS = 16
N_KV_HEADS = 4
GROUP = N_Q_HEADS // N_KV_HEADS
HEAD_DIM = 64
Q_WIDTH = N_Q_HEADS * HEAD_DIM
KV_WIDTH = N_KV_HEADS * HEAD_DIM
WINDOW = 128
BLOCK = 128
GMLP_WIDTH = 1024
GMLP_GROUPS = 8
GMLP_GROUP_DIM = GMLP_WIDTH // GMLP_GROUPS
D_FF = 4 * D_MODEL
LN_EPS = 1e-5
DEPTH = 1
DEEPNORM_ALPHA = (2 * DEPTH) ** 0.25

OFF_Q = 0
OFF_K = OFF_Q + Q_WIDTH
OFF_V = OFF_K + KV_WIDTH
OFF_Z = OFF_V + KV_WIDTH
OFF_GA = OFF_Z + 2 * GMLP_WIDTH
OFF_GG = OFF_GA + D_MODEL
IN_WIDTH = OFF_GG + D_MODEL
R_K = OFF_K
R_Z = OFF_Z
R_GA = OFF_GA
R_GG = OFF_GG
R_WIDTH = IN_WIDTH

TM = 256
FF_CHUNK = 512
P_CHUNK = 256
V7X_VMEM_LIMIT_BYTES = 60 * 1024 * 1024

BF16 = jnp.bfloat16
F32 = jnp.float32


def _dot(a, b):
    return jnp.dot(a, b, preferred_element_type=F32)


def _layer_norm(x, g, b):
    mu = jnp.mean(x, axis=-1, keepdims=True)
    xc = x - mu
    var = jnp.mean(xc * xc, axis=-1, keepdims=True)
    return xc * jax.lax.rsqrt(var + LN_EPS) * g + b


def _layer_kernel(tiles_per_seq,
                  sinks_ref, x_ref, w_qvt_ref, b_qt_ref, b_vt_ref, w_r_ref, b_r_ref,
                  lng_ref, lnb_ref, ws_ref, bs_ref, w_bra_ref, w_brg_ref, w_out_ref,
                  ln1g_ref, ln1b_ref, w_up_ref, w_down_ref, ln2g_ref, ln2b_ref,
                  o_ref, kprev_ref, vprev_ref, resid_ref, attn_ref, sg_ref):
    t = pl.program_id(0)
    n_blk = TM // BLOCK
    rd_slot = (t + 1) % 2
    wr_slot = t % 2
    has_prev_tile = (t % tiles_per_seq) > 0

    def token_mixer():
        x = x_ref[...]
        xb = x.astype(BF16)

        def proj(lo, hi):
            return _dot(xb, w_r_ref[:, lo:hi]) + b_r_ref[:, lo:hi]

        def gelu_chunk(c):
            return jax.nn.gelu(proj(R_Z + c * P_CHUNK, R_Z + (c + 1) * P_CHUNK), approximate=True)

        qv_t = jax.lax.dot_general(w_qvt_ref[...], xb, (((1,), (1,)), ((), ())),
                                   preferred_element_type=F32)
        q_t = ((qv_t[:Q_WIDTH, :] + b_qt_ref[...]) * (HEAD_DIM ** -0.5)).astype(BF16)
        v_t_new = (qv_t[Q_WIDTH:, :] + b_vt_ref[...]).astype(BF16)
        yield

        k_new = proj(R_K, R_K + KV_WIDTH).astype(BF16)
        k_prev = jnp.where(has_prev_tile, kprev_ref[rd_slot], jnp.zeros((BLOCK, KV_WIDTH), BF16))
        v_t_prev = jnp.where(has_prev_tile, vprev_ref[rd_slot], jnp.zeros((KV_WIDTH, BLOCK), BF16))
        k_all = jnp.concatenate([k_prev, k_new], axis=0)
        v_t_all = jnp.concatenate([v_t_prev, v_t_new], axis=1)
        kprev_ref[wr_slot] = k_new[TM - BLOCK:, :]
        vprev_ref[wr_slot] = v_t_new[:, TM - BLOCK:]
        n_z = 2 * GMLP_WIDTH // P_CHUNK
        n_lead = n_z // 2
        zs = [gelu_chunk(c) for c in range(n_lead // 2)]
        yield
        zs += [gelu_chunk(c) for c in range(n_lead // 2, n_lead)]
        yield

        k_pos = jax.lax.broadcasted_iota(jnp.int32, (2 * BLOCK, BLOCK), 0)
        q_pos = jax.lax.broadcasted_iota(jnp.int32, (2 * BLOCK, BLOCK), 1)
        diff = k_pos - q_pos
        band = (diff >= 1) & (diff <= WINDOW)
        pair_width = 2 * HEAD_DIM
        zero_rows = jnp.zeros((HEAD_DIM, GROUP * BLOCK), BF16)

        def scores_t(j, hk):
            pair, odd = divmod(hk, 2)
            toks = slice(j * BLOCK, (j + 1) * BLOCK)
            k_pair = k_all[j * BLOCK:(j + 2) * BLOCK, pair * pair_width:(pair + 1) * pair_width]
            q_cat = jnp.concatenate(
                [q_t[(hk * GROUP + g) * HEAD_DIM:(hk * GROUP + g + 1) * HEAD_DIM, toks]
                 for g in range(GROUP)], axis=1)
            rhs = jnp.concatenate([zero_rows, q_cat] if odd else [q_cat, zero_rows], axis=0)
            return _dot(k_pair, rhs)

        def softmax_pv(j, hk, sc_t):
            first_key = jnp.where(has_prev_tile, 0, BLOCK) if j == 0 else 0
            mask = band & (k_pos >= first_key)
            toks = slice(j * BLOCK, (j + 1) * BLOCK)
            p_parts, inv_parts = [], []
            for g in range(GROUP):
                sc = jnp.where(mask, sc_t[:, g * BLOCK:(g + 1) * BLOCK], -jnp.inf)
                sink = sinks_ref[hk * GROUP + g]
                m = jnp.maximum(jnp.max(sc, axis=0, keepdims=True), sink)
                p = jnp.exp(sc - m)
                denom = jnp.sum(p, axis=0, keepdims=True) + jnp.exp(sink - m)
                p_parts.append(p.astype(BF16))
                inv_parts.append(1.0 / denom)
            p_t = jnp.concatenate(p_parts, axis=1)
            v_t = v_t_all[hk * HEAD_DIM:(hk + 1) * HEAD_DIM, j * BLOCK:(j + 2) * BLOCK]
            o_t = _dot(v_t, p_t)
            for g in range(GROUP):
                feat = slice((hk * GROUP + g) * HEAD_DIM, (hk * GROUP + g + 1) * HEAD_DIM)
                attn_ref[feat, toks] = (o_t[:, g * BLOCK:(g + 1) * BLOCK]
                                        * inv_parts[g]).astype(BF16)

        side_work = [lambda c=c: gelu_chunk(c) for c in range(n_lead, n_z)]
        side_work += [lambda c=c: proj(R_GA + c * P_CHUNK, R_GA + (c + 1) * P_CHUNK)
                      for c in range(D_MODEL // P_CHUNK)]
        items = [(j, hk) for j in range(n_blk) for hk in range(N_KV_HEADS)]
        assert len(side_work) == len(items)
        side_out = []
        sc = [scores_t(*items[0]), scores_t(*items[1])]
        for i, item in enumerate(items):
            yield
            side_out.append(side_work[i]())
            softmax_pv(*item, sc[i])
            if i + 2 < len(items):
                sc.append(scores_t(*items[i + 2]))
        yield
        zs += side_out[:n_z - n_lead]
        u = jnp.concatenate(zs[:n_z // 2], axis=1)
        vz = jnp.concatenate(zs[n_z // 2:], axis=1)
        pre_a = jnp.concatenate(side_out[n_z - n_lead:], axis=1)

        pre_g = proj(R_GG, R_WIDTH)
        v = _layer_norm(vz, lng_ref[...], lnb_ref[...]).astype(BF16)
        g_a = jax.nn.sigmoid(pre_a)
        yield

        y_a = jax.lax.dot_general(attn_ref[...], w_bra_ref[...], (((0,), (0,)), ((), ())),
                                  preferred_element_type=F32)
        t_idx = jax.lax.broadcasted_iota(jnp.int32, (BLOCK, BLOCK), 0)
        s_idx = jax.lax.broadcasted_iota(jnp.int32, (BLOCK, BLOCK), 1)
        causal = s_idx <= t_idx
        for g in range(GMLP_GROUPS):
            w = jnp.where(causal, ws_ref[g], 0.0).astype(BF16)
            cols = slice(g * GMLP_GROUP_DIM, (g + 1) * GMLP_GROUP_DIM)
            for c in range(n_blk):
                rows = slice(c * BLOCK, (c + 1) * BLOCK)
                mixed = _dot(w, v[rows, cols]) + bs_ref[:, cols]
                sg_ref[rows, cols] = (u[rows, cols] * mixed).astype(BF16)
        g_g = jax.nn.sigmoid(pre_g)
        mix = g_a * y_a
        yield

        y_g = _dot(sg_ref[...], w_brg_ref[...])
        mix = (mix + g_g * y_g).astype(BF16)
        yield
        resid_ref[wr_slot] = DEEPNORM_ALPHA * x + _dot(mix, w_out_ref[...])
        yield

    def channel_mixer():
        resid = jnp.where(t > 0, resid_ref[rd_slot], 0.0)
        x1 = _layer_norm(resid, ln1g_ref[...], ln1b_ref[...])
        x1b = x1.astype(BF16)
        yield
        n_chunks = D_FF // FF_CHUNK
        acc = None
        for c in range(n_chunks):
            cols = slice(c * FF_CHUNK, (c + 1) * FF_CHUNK)
            h = jnp.maximum(_dot(x1b, w_up_ref[:, cols]), 0.0)
            h = (h * h).astype(BF16)
            yield
            if c < n_chunks - 1:
                part = _dot(h, w_down_ref[cols, :])
                acc = part if acc is None else acc + part
                yield
        for r in range(n_blk):
            rows = slice(r * BLOCK, (r + 1) * BLOCK)
            y = acc[rows, :] + _dot(h[rows, :], w_down_ref[cols, :])
            o_ref[rows, :] = _layer_norm(DEEPNORM_ALPHA * x1[rows, :] + y,
                                         ln2g_ref[...], ln2b_ref[...])
            yield

    tm_units = token_mixer()
    cm_units = channel_mixer()
    after = {}
    for i, u_idx in enumerate(CM_AFTER):
        after.setdefault(u_idx, []).append(i)
    n_tm = 0
    while next(tm_units, "done") != "done":
        for _ in after.pop(n_tm, []):
            next(cm_units)
        n_tm += 1
    leftover = next(cm_units, "done")
    assert not after and leftover == "done", (after, n_tm)


CM_AFTER = (0, 1, 2, 3, 4, 5, 6, 7, 8, 9, 10, 11, 12, 12, 13, 13, 14, 14)


def _resident(shape):
    return pl.BlockSpec(shape, lambda *_: (0,) * len(shape), pipeline_mode=pl.Buffered(1))


def _layer_call(x2d, tiles_per_seq, sinks, w_qvt, b_qt, b_vt, w_r, b_r, lng, lnb, w_s, bs_tile,
                w_bra, w_brg, w_out, ln1g, ln1b, w_up, w_down, ln2g, ln2b):
    n_tok, d = x2d.shape
    assert d == D_MODEL and n_tok % TM == 0 and TM % BLOCK == 0
    n_tiles = n_tok // TM
    return pl.pallas_call(
        functools.partial(_layer_kernel, tiles_per_seq),
        out_shape=jax.ShapeDtypeStruct(x2d.shape, F32),
        grid=(n_tiles + 1,),
        in_specs=[
            pl.BlockSpec(memory_space=pltpu.SMEM),
            pl.BlockSpec((TM, D_MODEL), lambda t: (jnp.minimum(t, n_tiles - 1), 0)),
            _resident((Q_WIDTH + KV_WIDTH, D_MODEL)),
            _resident((Q_WIDTH, TM)),
            _resident((KV_WIDTH, TM)),
            _resident((D_MODEL, IN_WIDTH)),
            _resident((1, IN_WIDTH)),
            _resident((1, GMLP_WIDTH)),
            _resident((1, GMLP_WIDTH)),
            _resident((GMLP_GROUPS, BLOCK, BLOCK)),
            _resident((BLOCK, GMLP_WIDTH)),
            _resident((Q_WIDTH, D_MODEL)),
            _resident((GMLP_WIDTH, D_MODEL)),
            _resident((D_MODEL, D_MODEL)),
            _resident((1, D_MODEL)),
            _resident((1, D_MODEL)),
            _resident((D_MODEL, D_FF)),
            _resident((D_FF, D_MODEL)),
            _resident((1, D_MODEL)),
            _resident((1, D_MODEL)),
        ],
        out_specs=pl.BlockSpec((TM, D_MODEL), lambda t: (jnp.maximum(t - 1, 0), 0)),
        scratch_shapes=[
            pltpu.VMEM((2, BLOCK, KV_WIDTH), BF16),
            pltpu.VMEM((2, KV_WIDTH, BLOCK), BF16),
            pltpu.VMEM((2, TM, D_MODEL), F32),
            pltpu.VMEM((Q_WIDTH, TM), BF16),
            pltpu.VMEM((TM, GMLP_WIDTH), BF16),
        ],
        compiler_params=pltpu.CompilerParams(
            dimension_semantics=("arbitrary",),
            vmem_limit_bytes=V7X_VMEM_LIMIT_BYTES),
        name="decoder_layer",
    )(sinks, x2d, w_qvt, b_qt, b_vt, w_r, b_r, lng, lnb, w_s, bs_tile, w_bra, w_brg, w_out,
      ln1g, ln1b, w_up, w_down, ln2g, ln2b)


def kernel(x, w_in, b_in, attn_sinks, gmlp_ln_g, gmlp_ln_b, gmlp_w_s, gmlp_b_s, w_branch_attn,
           w_branch_gmlp, w_out, ln1_g, ln1_b, w_up, w_down, ln2_g, ln2_b):
    assert w_in.shape[0] == DEPTH
    bsz, seq, d = x.shape
    assert seq % TM == 0
    x2d = x.reshape(bsz * seq, d)
    for l in range(DEPTH):
        bs_tile = jnp.repeat(gmlp_b_s[l].T, GMLP_GROUP_DIM, axis=1)
        w_qvt = jnp.concatenate([w_in[l][:, OFF_Q:OFF_K], w_in[l][:, OFF_V:OFF_Z]], axis=1).T
        b_qt = jnp.broadcast_to(b_in[l][OFF_Q:OFF_K, None], (Q_WIDTH, TM))
        b_vt = jnp.broadcast_to(b_in[l][OFF_V:OFF_Z, None], (KV_WIDTH, TM))
        x2d = _layer_call(
            x2d, seq // TM, attn_sinks[l], w_qvt.astype(BF16), b_qt, b_vt,
            w_in[l].astype(BF16), b_in[l][None, :],
            gmlp_ln_g[l][None, :], gmlp_ln_b[l][None, :], gmlp_w_s[l], bs_tile,
            w_branch_attn[l].astype(BF16), w_branch_gmlp[l].astype(BF16), w_out[l].astype(BF16),
            ln1_g[l][None, :], ln1_b[l][None, :], w_up[l].astype(BF16), w_down[l].astype(BF16),
            ln2_g[l][None, :], ln2_b[l][None, :])
    return x2d.reshape(bsz, seq, d)
```

```python
import functools

import jax
import jax.numpy as jnp
from jax.experimental import pallas as pl
from jax.experimental.pallas import tpu as pltpu

D_MODEL = 1024
N_Q_HEADS = 16
N_KV_HEADS = 4
GROUP = N_Q_HEADS // N_KV_HEADS
HEAD_DIM = 64
Q_WIDTH = N_Q_HEADS * HEAD_DIM
KV_WIDTH = N_KV_HEADS * HEAD_DIM
WINDOW = 128
BLOCK = 128
GMLP_WIDTH = 1024
GMLP_GROUPS = 8
GMLP_GROUP_DIM = GMLP_WIDTH // GMLP_GROUPS
D_FF = 4 * D_MODEL
LN_EPS = 1e-5
DEPTH = 1
DEEPNORM_ALPHA = (2 * DEPTH) ** 0.25

OFF_Q = 0
OFF_K = OFF_Q + Q_WIDTH
OFF_V = OFF_K + KV_WIDTH
OFF_Z = OFF_V + KV_WIDTH
OFF_GA = OFF_Z + 2 * GMLP_WIDTH
OFF_GG = OFF_GA + D_MODEL
IN_WIDTH = OFF_GG + D_MODEL
R_K = 0
R_Z = R_K + KV_WIDTH
R_GA = R_Z + 2 * GMLP_WIDTH
R_GG = R_GA + D_MODEL
R_WIDTH = R_GG + D_MODEL

TM = 256
FF_CHUNK = 512
P_CHUNK = 256
CONVERT_CHUNK_BYTES = 2 * 1024 * 1024
V7X_VMEM_LIMIT_BYTES = 60 * 1024 * 1024

BF16 = jnp.bfloat16
F32 = jnp.float32


def _dot(a, b):
    return jnp.dot(a, b, preferred_element_type=F32)


def _layer_norm(x, g, b):
    mu = jnp.mean(x, axis=-1, keepdims=True)
    xc = x - mu
    var = jnp.mean(xc * xc, axis=-1, keepdims=True)
    return xc * jax.lax.rsqrt(var + LN_EPS) * g + b


def _convert_weight(src_hbm, src_col0, width, dst_ref, dst_col0):
    n_rows = src_hbm.shape[0]
    rows = min(n_rows, CONVERT_CHUNK_BYTES // (4 * width))
    assert n_rows % rows == 0 and rows % 16 == 0
    n_chunks = n_rows // rows

    def body(stage, sem):
        def chunk_copy(i, slot):
            return pltpu.make_async_copy(
                src_hbm.at[pl.ds(i * rows, rows), pl.ds(src_col0, width)],
                stage.at[slot], sem.at[slot])

        chunk_copy(0, 0).start()

        def step(i, carry):
            slot = i % 2

            @pl.when(i + 1 < n_chunks)
            def _():
                chunk_copy(i + 1, 1 - slot).start()

            chunk_copy(i, slot).wait()
            row0 = pl.multiple_of(i * rows, rows)
            dst_ref[pl.ds(row0, rows), dst_col0:dst_col0 + width] = stage[slot].astype(BF16)
            return carry

        jax.lax.fori_loop(0, n_chunks, step, 0)

    pl.run_scoped(body, pltpu.VMEM((2, rows, width), F32), pltpu.SemaphoreType.DMA((2,)))


def _layer_kernel(tiles_per_seq,
                  sinks_ref, x_ref, w_qvt_ref, b_qt_ref, b_vt_ref, b_r_ref,
                  lng_ref, lnb_ref, ws_ref, bs_ref, ln1g_ref, ln1b_ref, ln2g_ref, ln2b_ref,
                  w_in_hbm, w_bra_hbm, w_brg_hbm, w_out_hbm, w_up_hbm, w_down_hbm,
                  o_ref, kprev_ref, vprev_ref, resid_ref, attn_ref, sg_ref,
                  w_r_ref, w_bra_ref, w_brg_ref, w_out_ref, w_up_ref, w_down_ref):
    t = pl.program_id(0)

    @pl.when(t == 0)
    def _():
        _convert_weight(w_in_hbm, OFF_K, KV_WIDTH, w_r_ref, R_K)
        _convert_weight(w_in_hbm, OFF_Z, IN_WIDTH - OFF_Z, w_r_ref, R_Z)
        _convert_weight(w_bra_hbm, 0, D_MODEL, w_bra_ref, 0)
        _convert_weight(w_brg_hbm, 0, D_MODEL, w_brg_ref, 0)
        _convert_weight(w_out_hbm, 0, D_MODEL, w_out_ref, 0)
        _convert_weight(w_up_hbm, 0, D_FF, w_up_ref, 0)
        _convert_weight(w_down_hbm, 0, D_MODEL, w_down_ref, 0)

    n_blk = TM // BLOCK
    rd_slot = (t + 1) % 2
    wr_slot = t % 2
    has_prev_tile = (t % tiles_per_seq) > 0

    def token_mixer():
        x = x_ref[...]
        xb = x.astype(BF16)

        def proj(lo, hi):
            return _dot(xb, w_r_ref[:, lo:hi]) + b_r_ref[:, lo:hi]

        def gelu_chunk(c):
            return jax.nn.gelu(proj(R_Z + c * P_CHUNK, R_Z + (c + 1) * P_CHUNK), approximate=True)

        qv_t = jax.lax.dot_general(w_qvt_ref[...], xb, (((1,), (1,)), ((), ())),
                                   preferred_element_type=F32)
        q_t = ((qv_t[:Q_WIDTH, :] + b_qt_ref[...]) * (HEAD_DIM ** -0.5)).astype(BF16)
        v_t_new = (qv_t[Q_WIDTH:, :] + b_vt_ref[...]).astype(BF16)
        yield

        k_new = proj(R_K, R_Z).astype(BF16)
        k_prev = jnp.where(has_prev_tile, kprev_ref[rd_slot], jnp.zeros((BLOCK, KV_WIDTH), BF16))
        v_t_prev = jnp.where(has_prev_tile, vprev_ref[rd_slot], jnp.zeros((KV_WIDTH, BLOCK), BF16))
        k_all = jnp.concatenate([k_prev, k_new], axis=0)
        v_t_all = jnp.concatenate([v_t_prev, v_t_new], axis=1)
        kprev_ref[wr_slot] = k_new[TM - BLOCK:, :]
        vprev_ref[wr_slot] = v_t_new[:, TM - BLOCK:]
        n_z = 2 * GMLP_WIDTH // P_CHUNK
        n_lead = n_z // 2
        zs = [gelu_chunk(c) for c in range(n_lead // 2)]
        yield
        zs += [gelu_chunk(c) for c in range(n_lead // 2, n_lead)]
        yield

        k_pos = jax.lax.broadcasted_iota(jnp.int32, (2 * BLOCK, BLOCK), 0)
        q_pos = jax.lax.broadcasted_iota(jnp.int32, (2 * BLOCK, BLOCK), 1)
        diff = k_pos - q_pos
        band = (diff >= 1) & (diff <= WINDOW)
        pair_width = 2 * HEAD_DIM
        zero_rows = jnp.zeros((HEAD_DIM, GROUP * BLOCK), BF16)

        def scores_t(j, hk):
            pair, odd = divmod(hk, 2)
            toks = slice(j * BLOCK, (j + 1) * BLOCK)
            k_pair = k_all[j * BLOCK:(j + 2) * BLOCK, pair * pair_width:(pair + 1) * pair_width]
            q_cat = jnp.concatenate(
                [q_t[(hk * GROUP + g) * HEAD_DIM:(hk * GROUP + g + 1) * HEAD_DIM, toks]
                 for g in range(GROUP)], axis=1)
            rhs = jnp.concatenate([zero_rows, q_cat] if odd else [q_cat, zero_rows], axis=0)
            return _dot(k_pair, rhs)

        def softmax(j, hk, sc_t):
            first_key = jnp.where(has_prev_tile, 0, BLOCK) if j == 0 else 0
            mask = band & (k_pos >= first_key)
            toks = slice(j * BLOCK, (j + 1) * BLOCK)
            p_parts, inv_parts = [], []
            for g in range(GROUP):
                sc = jnp.where(mask, sc_t[:, g * BLOCK:(g + 1) * BLOCK], -jnp.inf)
                sink = sinks_ref[hk * GROUP + g]
                m = jnp.maximum(jnp.max(sc, axis=0, keepdims=True), sink)
                p = jnp.exp(sc - m)
                denom = jnp.sum(p, axis=0, keepdims=True) + jnp.exp(sink - m)
                p_parts.append(p.astype(BF16))
                inv_parts.append(1.0 / denom)
            return jnp.concatenate(p_parts, axis=1), inv_parts

        def pv(j, hk, p_t, inv_parts):
            toks = slice(j * BLOCK, (j + 1) * BLOCK)
            v_t = v_t_all[hk * HEAD_DIM:(hk + 1) * HEAD_DIM, j * BLOCK:(j + 2) * BLOCK]
            o_t = _dot(v_t, p_t)
            for g in range(GROUP):
                feat = slice((hk * GROUP + g) * HEAD_DIM, (hk * GROUP + g + 1) * HEAD_DIM)
                attn_ref[feat, toks] = (o_t[:, g * BLOCK:(g + 1) * BLOCK]
                                        * inv_parts[g]).astype(BF16)

        side_work = [lambda c=c: gelu_chunk(c) for c in range(n_lead, n_z)]
        side_work += [lambda c=c: proj(R_GA + c * P_CHUNK, R_GA + (c + 1) * P_CHUNK)
                      for c in range(D_MODEL // P_CHUNK)]
        items = [(j, hk) for j in range(n_blk) for hk in range(N_KV_HEADS)]
        assert len(side_work) == len(items)
        side_out = []
        sc = [scores_t(*items[0]), scores_t(*items[1])]
        for i, item in enumerate(items):
            yield
            probs = softmax(*item, sc[i])
            side_out.append(side_work[i]())
            pv(*item, *probs)
            if i + 2 < len(items):
                sc.append(scores_t(*items[i + 2]))
        yield
        zs += side_out[:n_z - n_lead]
        u = jnp.concatenate(zs[:n_z // 2], axis=1)
        vz = jnp.concatenate(zs[n_z // 2:], axis=1)
        pre_a = jnp.concatenate(side_out[n_z - n_lead:], axis=1)

        pre_g = proj(R_GG, R_WIDTH)
        v = _layer_norm(vz, lng_ref[...], lnb_ref[...]).astype(BF16)
        g_a = jax.nn.sigmoid(pre_a)
        yield

        y_a = jax.lax.dot_general(attn_ref[...], w_bra_ref[...], (((0,), (0,)), ((), ())),
                                  preferred_element_type=F32)
        t_idx = jax.lax.broadcasted_iota(jnp.int32, (BLOCK, BLOCK), 0)
        s_idx = jax.lax.broadcasted_iota(jnp.int32, (BLOCK, BLOCK), 1)
        causal = s_idx <= t_idx
        for g in range(GMLP_GROUPS):
            w = jnp.where(causal, ws_ref[g], 0.0).astype(BF16)
            cols = slice(g * GMLP_GROUP_DIM, (g + 1) * GMLP_GROUP_DIM)
            for c in range(n_blk):
                rows = slice(c * BLOCK, (c + 1) * BLOCK)
                mixed = _dot(w, v[rows, cols]) + bs_ref[:, cols]
                sg_ref[rows, cols] = (u[rows, cols] * mixed).astype(BF16)
        g_g = jax.nn.sigmoid(pre_g)
        mix = g_a * y_a
        yield

        y_g = _dot(sg_ref[...], w_brg_ref[...])
        mix = (mix + g_g * y_g).astype(BF16)
        yield
        resid_ref[wr_slot] = DEEPNORM_ALPHA * x + _dot(mix, w_out_ref[...])
        yield

    def channel_mixer():
        resid = jnp.where(t > 0, resid_ref[rd_slot], 0.0)
        x1 = _layer_norm(resid, ln1g_ref[...], ln1b_ref[...])
        x1b = x1.astype(BF16)
        yield
        n_chunks = D_FF // FF_CHUNK
        acc = None
        for c in range(n_chunks):
            cols = slice(c * FF_CHUNK, (c + 1) * FF_CHUNK)
            h = jnp.maximum(_dot(x1b, w_up_ref[:, cols]), 0.0)
            h = (h * h).astype(BF16)
            yield
            if c < n_chunks - 1:
                part = _dot(h, w_down_ref[cols, :])
                acc = part if acc is None else acc + part
                yield
        for r in range(n_blk):
            rows = slice(r * BLOCK, (r + 1) * BLOCK)
            y = acc[rows, :] + _dot(h[rows, :], w_down_ref[cols, :])
            o_ref[rows, :] = _layer_norm(DEEPNORM_ALPHA * x1[rows, :] + y,
                                         ln2g_ref[...], ln2b_ref[...])
            yield

    tm_units = token_mixer()
    cm_units = channel_mixer()
    after = {}
    for i, u_idx in enumerate(CM_AFTER):
        after.setdefault(u_idx, []).append(i)
    n_tm = 0
    while next(tm_units, "done") != "done":
        for _ in after.pop(n_tm, []):
            next(cm_units)
        n_tm += 1
    leftover = next(cm_units, "done")
    assert not after and leftover == "done", (after, n_tm)


CM_AFTER = (0, 1, 2, 3, 4, 5, 6, 7, 8, 9, 10, 11, 12, 12, 13, 13, 14, 14)


def _resident(shape):
    return pl.BlockSpec(shape, lambda *_: (0,) * len(shape), pipeline_mode=pl.Buffered(1))


def _layer_call(x2d, tiles_per_seq, sinks, w_qvt, b_qt, b_vt, b_r, lng, lnb, w_s, bs_tile,
                ln1g, ln1b, ln2g, ln2b, w_in, w_bra, w_brg, w_out, w_up, w_down):
    n_tok, d = x2d.shape
    assert d == D_MODEL and n_tok % TM == 0 and TM % BLOCK == 0
    n_tiles = n_tok // TM
    in_hbm = pl.BlockSpec(memory_space=pl.ANY)
    return pl.pallas_call(
        functools.partial(_layer_kernel, tiles_per_seq),
        out_shape=jax.ShapeDtypeStruct(x2d.shape, F32),
        grid=(n_tiles + 1,),
        in_specs=[
            pl.BlockSpec(memory_space=pltpu.SMEM),
            pl.BlockSpec((TM, D_MODEL), lambda t: (jnp.minimum(t, n_tiles - 1), 0)),
            _resident((Q_WIDTH + KV_WIDTH, D_MODEL)),
            _resident((Q_WIDTH, TM)),
            _resident((KV_WIDTH, TM)),
            _resident((1, R_WIDTH)),
            _resident((1, GMLP_WIDTH)),
            _resident((1, GMLP_WIDTH)),
            _resident((GMLP_GROUPS, BLOCK, BLOCK)),
            _resident((BLOCK, GMLP_WIDTH)),
            _resident((1, D_MODEL)),
            _resident((1, D_MODEL)),
            _resident((1, D_MODEL)),
            _resident((1, D_MODEL)),
            in_hbm, in_hbm, in_hbm, in_hbm, in_hbm, in_hbm,
        ],
        out_specs=pl.BlockSpec((TM, D_MODEL), lambda t: (jnp.maximum(t - 1, 0), 0)),
        scratch_shapes=[
            pltpu.VMEM((2, BLOCK, KV_WIDTH), BF16),
            pltpu.VMEM((2, KV_WIDTH, BLOCK), BF16),
            pltpu.VMEM((2, TM, D_MODEL), F32),
            pltpu.VMEM((Q_WIDTH, TM), BF16),
            pltpu.VMEM((TM, GMLP_WIDTH), BF16),
            pltpu.VMEM((D_MODEL, R_WIDTH), BF16),
            pltpu.VMEM((Q_WIDTH, D_MODEL), BF16),
            pltpu.VMEM((GMLP_WIDTH, D_MODEL), BF16),
            pltpu.VMEM((D_MODEL, D_MODEL), BF16),
            pltpu.VMEM((D_MODEL, D_FF), BF16),
            pltpu.VMEM((D_FF, D_MODEL), BF16),
        ],
        compiler_params=pltpu.CompilerParams(
            dimension_semantics=("arbitrary",),
            vmem_limit_bytes=V7X_VMEM_LIMIT_BYTES),
        name="decoder_layer",
    )(sinks, x2d, w_qvt, b_qt, b_vt, b_r, lng, lnb, w_s, bs_tile, ln1g, ln1b, ln2g, ln2b,
      w_in, w_bra, w_brg, w_out, w_up, w_down)


def kernel(x, w_in, b_in, attn_sinks, gmlp_ln_g, gmlp_ln_b, gmlp_w_s, gmlp_b_s, w_branch_attn,
           w_branch_gmlp, w_out, ln1_g, ln1_b, w_up, w_down, ln2_g, ln2_b):
    assert w_in.shape[0] == DEPTH
    bsz, seq, d = x.shape
    assert seq % TM == 0
    x2d = x.reshape(bsz * seq, d)
    for l in range(DEPTH):
        bs_tile = jnp.repeat(gmlp_b_s[l].T, GMLP_GROUP_DIM, axis=1)
        w_qvt = jnp.concatenate([w_in[l][:, OFF_Q:OFF_K], w_in[l][:, OFF_V:OFF_Z]], axis=1).T
        b_qt = jnp.broadcast_to(b_in[l][OFF_Q:OFF_K, None], (Q_WIDTH, TM))
        b_vt = jnp.broadcast_to(b_in[l][OFF_V:OFF_Z, None], (KV_WIDTH, TM))
        b_r = jnp.concatenate([b_in[l][OFF_K:OFF_V], b_in[l][OFF_Z:]])[None, :]
        x2d = _layer_call(
            x2d, seq // TM, attn_sinks[l], w_qvt.astype(BF16), b_qt, b_vt, b_r,
            gmlp_ln_g[l][None, :], gmlp_ln_b[l][None, :], gmlp_w_s[l], bs_tile,
            ln1_g[l][None, :], ln1_b[l][None, :], ln2_g[l][None, :], ln2_b[l][None, :],
            w_in[l], w_branch_attn[l], w_branch_gmlp[l], w_out[l], w_up[l], w_down[l])
    return x2d.reshape(bsz, seq, d)
```

```python
import functools

import jax
import jax.numpy as jnp
from jax.experimental import pallas as pl
from jax.experimental.pallas import tpu as pltpu

D_MODEL = 1024
N_Q_HEADS = 16
N_KV_HEADS = 4
GROUP = N_Q_HEADS // N_KV_HEADS
HEAD_DIM = 64
Q_WIDTH = N_Q_HEADS * HEAD_DIM
KV_WIDTH = N_KV_HEADS * HEAD_DIM
WINDOW = 128
BLOCK = 128
GMLP_WIDTH = 1024
GMLP_GROUPS = 8
GMLP_GROUP_DIM = GMLP_WIDTH // GMLP_GROUPS
D_FF = 4 * D_MODEL
LN_EPS = 1e-5
DEPTH = 1
DEEPNORM_ALPHA = (2 * DEPTH) ** 0.25

OFF_Q = 0
OFF_K = OFF_Q + Q_WIDTH
OFF_V = OFF_K + KV_WIDTH
OFF_Z = OFF_V + KV_WIDTH
OFF_GA = OFF_Z + 2 * GMLP_WIDTH
OFF_GG = OFF_GA + D_MODEL
IN_WIDTH = OFF_GG + D_MODEL
R_K = 0
R_Z = R_K + KV_WIDTH
R_GA = R_Z + 2 * GMLP_WIDTH
R_GG = R_GA + D_MODEL
R_WIDTH = R_GG + D_MODEL

TM = 256
FF_CHUNK = 512
P_CHUNK = 256
CONVERT_ROWS, CONVERT_COLS = 256, 1024
CONVERT_SLOTS = 4
V7X_VMEM_LIMIT_BYTES = 60 * 1024 * 1024

BF16 = jnp.bfloat16
F32 = jnp.float32


def _dot(a, b):
    return jnp.dot(a, b, preferred_element_type=F32)


def _layer_norm(x, g, b):
    mu = jnp.mean(x, axis=-1, keepdims=True)
    xc = x - mu
    var = jnp.mean(xc * xc, axis=-1, keepdims=True)
    return xc * jax.lax.rsqrt(var + LN_EPS) * g + b


def _convert_weights(jobs, block_rows, block_cols):
    blocks = []
    for src_hbm, src_col0, dst_ref, dst_col0, n_cols in jobs:
        n_rows = src_hbm.shape[0]
        assert n_rows % block_rows == 0 and n_cols % block_cols == 0
        for r0 in range(0, n_rows, block_rows):
            for c0 in range(0, n_cols, block_cols):
                blocks.append((src_hbm, r0, src_col0 + c0, dst_ref, dst_col0 + c0))

    def body(stage, sem):
        def block_copy(i):
            src_hbm, r0, src_c, _, _ = blocks[i]
            slot = i % CONVERT_SLOTS
            return pltpu.make_async_copy(
                src_hbm.at[pl.ds(r0, block_rows), pl.ds(src_c, block_cols)],
                stage.at[slot], sem.at[slot])

        for i in range(min(CONVERT_SLOTS - 1, len(blocks))):
            block_copy(i).start()
        for i, (_, r0, _, dst_ref, dst_c) in enumerate(blocks):
            block_copy(i).wait()
            dst_ref[r0:r0 + block_rows, dst_c:dst_c + block_cols] = (
                stage[i % CONVERT_SLOTS].astype(BF16))
            nxt = i + CONVERT_SLOTS - 1
            if nxt < len(blocks):
                block_copy(nxt).start()

    pl.run_scoped(body, pltpu.VMEM((CONVERT_SLOTS, block_rows, block_cols), F32),
                  pltpu.SemaphoreType.DMA((CONVERT_SLOTS,)))


def _layer_kernel(tiles_per_seq,
                  sinks_ref, x_ref, w_qvt_ref, b_qt_ref, b_vt_ref, b_r_ref,
                  lng_ref, lnb_ref, ws_ref, bs_ref, ln1g_ref, ln1b_ref, ln2g_ref, ln2b_ref,
                  w_in_hbm, w_bra_hbm, w_brg_hbm, w_out_hbm, w_up_hbm, w_down_hbm,
                  o_ref, kprev_ref, vprev_ref, resid_ref, attn_ref, sg_ref,
                  w_r_ref, w_bra_ref, w_brg_ref, w_out_ref, w_up_ref, w_down_ref):
    t = pl.program_id(0)

    @pl.when(t == 0)
    def _():
        _convert_weights([(w_in_hbm, OFF_K, w_r_ref, R_K, KV_WIDTH)], D_MODEL, KV_WIDTH)
        _convert_weights([
            (w_in_hbm, OFF_Z, w_r_ref, R_Z, IN_WIDTH - OFF_Z),
            (w_bra_hbm, 0, w_bra_ref, 0, D_MODEL),
            (w_brg_hbm, 0, w_brg_ref, 0, D_MODEL),
            (w_out_hbm, 0, w_out_ref, 0, D_MODEL),
            (w_up_hbm, 0, w_up_ref, 0, D_FF),
            (w_down_hbm, 0, w_down_ref, 0, D_MODEL),
        ], CONVERT_ROWS, CONVERT_COLS)

    n_blk = TM // BLOCK
    rd_slot = (t + 1) % 2
    wr_slot = t % 2
    has_prev_tile = (t % tiles_per_seq) > 0

    def token_mixer():
        x = x_ref[...]
        xb = x.astype(BF16)

        def proj(lo, hi):
            return _dot(xb, w_r_ref[:, lo:hi]) + b_r_ref[:, lo:hi]

        def gelu_chunk(c):
            return jax.nn.gelu(proj(R_Z + c * P_CHUNK, R_Z + (c + 1) * P_CHUNK), approximate=True)

        qv_t = jax.lax.dot_general(w_qvt_ref[...], xb, (((1,), (1,)), ((), ())),
                                   preferred_element_type=F32)
        q_t = ((qv_t[:Q_WIDTH, :] + b_qt_ref[...]) * (HEAD_DIM ** -0.5)).astype(BF16)
        v_t_new = (qv_t[Q_WIDTH:, :] + b_vt_ref[...]).astype(BF16)
        yield

        k_new = proj(R_K, R_Z).astype(BF16)
        k_prev = jnp.where(has_prev_tile, kprev_ref[rd_slot], jnp.zeros((BLOCK, KV_WIDTH), BF16))
        v_t_prev = jnp.where(has_prev_tile, vprev_ref[rd_slot], jnp.zeros((KV_WIDTH, BLOCK), BF16))
        k_all = jnp.concatenate([k_prev, k_new], axis=0)
        v_t_all = jnp.concatenate([v_t_prev, v_t_new], axis=1)
        kprev_ref[wr_slot] = k_new[TM - BLOCK:, :]
        vprev_ref[wr_slot] = v_t_new[:, TM - BLOCK:]
        n_z = 2 * GMLP_WIDTH // P_CHUNK
        n_lead = n_z // 2
        zs = [gelu_chunk(c) for c in range(n_lead // 2)]
        yield
        zs += [gelu_chunk(c) for c in range(n_lead // 2, n_lead)]
        yield

        k_pos = jax.lax.broadcasted_iota(jnp.int32, (2 * BLOCK, BLOCK), 0)
        q_pos = jax.lax.broadcasted_iota(jnp.int32, (2 * BLOCK, BLOCK), 1)
        diff = k_pos - q_pos
        band = (diff >= 1) & (diff <= WINDOW)
        pair_width = 2 * HEAD_DIM
        zero_rows = jnp.zeros((HEAD_DIM, GROUP * BLOCK), BF16)

        def scores_t(j, hk):
            pair, odd = divmod(hk, 2)
            toks = slice(j * BLOCK, (j + 1) * BLOCK)
            k_pair = k_all[j * BLOCK:(j + 2) * BLOCK, pair * pair_width:(pair + 1) * pair_width]
            q_cat = jnp.concatenate(
                [q_t[(hk * GROUP + g) * HEAD_DIM:(hk * GROUP + g + 1) * HEAD_DIM, toks]
                 for g in range(GROUP)], axis=1)
            rhs = jnp.concatenate([zero_rows, q_cat] if odd else [q_cat, zero_rows], axis=0)
            return _dot(k_pair, rhs)

        def softmax(j, hk, sc_t):
            first_key = jnp.where(has_prev_tile, 0, BLOCK) if j == 0 else 0
            mask = band & (k_pos >= first_key)
            toks = slice(j * BLOCK, (j + 1) * BLOCK)
            p_parts, inv_parts = [], []
            for g in range(GROUP):
                sc = jnp.where(mask, sc_t[:, g * BLOCK:(g + 1) * BLOCK], -jnp.inf)
                sink = sinks_ref[hk * GROUP + g]
                m = jnp.maximum(jnp.max(sc, axis=0, keepdims=True), sink)
                p = jnp.exp(sc - m)
                denom = jnp.sum(p, axis=0, keepdims=True) + jnp.exp(sink - m)
                p_parts.append(p.astype(BF16))
                inv_parts.append(1.0 / denom)
            return jnp.concatenate(p_parts, axis=1), inv_parts

        def pv(j, hk, p_t, inv_parts):
            toks = slice(j * BLOCK, (j + 1) * BLOCK)
            v_t = v_t_all[hk * HEAD_DIM:(hk + 1) * HEAD_DIM, j * BLOCK:(j + 2) * BLOCK]
            o_t = _dot(v_t, p_t)
            for g in range(GROUP):
                feat = slice((hk * GROUP + g) * HEAD_DIM, (hk * GROUP + g + 1) * HEAD_DIM)
                attn_ref[feat, toks] = (o_t[:, g * BLOCK:(g + 1) * BLOCK]
                                        * inv_parts[g]).astype(BF16)

        side_work = [lambda c=c: gelu_chunk(c) for c in range(n_lead, n_z)]
        side_work += [lambda c=c: proj(R_GA + c * P_CHUNK, R_GA + (c + 1) * P_CHUNK)
                      for c in range(D_MODEL // P_CHUNK)]
        items = [(j, hk) for j in range(n_blk) for hk in range(N_KV_HEADS)]
        assert len(side_work) == len(items)
        side_out = []
        sc = [scores_t(*items[0]), scores_t(*items[1])]
        for i, item in enumerate(items):
            yield
            probs = softmax(*item, sc[i])
            side_out.append(side_work[i]())
            pv(*item, *probs)
            if i + 2 < len(items):
                sc.append(scores_t(*items[i + 2]))
        yield
        zs += side_out[:n_z - n_lead]
        u = jnp.concatenate(zs[:n_z // 2], axis=1)
        vz = jnp.concatenate(zs[n_z // 2:], axis=1)
        pre_a = jnp.concatenate(side_out[n_z - n_lead:], axis=1)

        pre_g = proj(R_GG, R_WIDTH)
        v = _layer_norm(vz, lng_ref[...], lnb_ref[...]).astype(BF16)
        g_a = jax.nn.sigmoid(pre_a)
        yield

        y_a = jax.lax.dot_general(attn_ref[...], w_bra_ref[...], (((0,), (0,)), ((), ())),
                                  preferred_element_type=F32)
        t_idx = jax.lax.broadcasted_iota(jnp.int32, (BLOCK, BLOCK), 0)
        s_idx = jax.lax.broadcasted_iota(jnp.int32, (BLOCK, BLOCK), 1)
        causal = s_idx <= t_idx
        for g in range(GMLP_GROUPS):
            w = jnp.where(causal, ws_ref[g], 0.0).astype(BF16)
            cols = slice(g * GMLP_GROUP_DIM, (g + 1) * GMLP_GROUP_DIM)
            for c in range(n_blk):
                rows = slice(c * BLOCK, (c + 1) * BLOCK)
                mixed = _dot(w, v[rows, cols]) + bs_ref[:, cols]
                sg_ref[rows, cols] = (u[rows, cols] * mixed).astype(BF16)
        g_g = jax.nn.sigmoid(pre_g)
        mix = g_a * y_a
        yield

        y_g = _dot(sg_ref[...], w_brg_ref[...])
        mix = (mix + g_g * y_g).astype(BF16)
        yield
        resid_ref[wr_slot] = DEEPNORM_ALPHA * x + _dot(mix, w_out_ref[...])
        yield

    def channel_mixer():
        resid = jnp.where(t > 0, resid_ref[rd_slot], 0.0)
        x1 = _layer_norm(resid, ln1g_ref[...], ln1b_ref[...])
        x1b = x1.astype(BF16)
        yield
        n_chunks = D_FF // FF_CHUNK
        acc = None
        for c in range(n_chunks):
            cols = slice(c * FF_CHUNK, (c + 1) * FF_CHUNK)
            h = jnp.maximum(_dot(x1b, w_up_ref[:, cols]), 0.0)
            h = (h * h).astype(BF16)
            yield
            if c < n_chunks - 1:
                part = _dot(h, w_down_ref[cols, :])
                acc = part if acc is None else acc + part
                yield
        for r in range(n_blk):
            rows = slice(r * BLOCK, (r + 1) * BLOCK)
            y = acc[rows, :] + _dot(h[rows, :], w_down_ref[cols, :])
            o_ref[rows, :] = _layer_norm(DEEPNORM_ALPHA * x1[rows, :] + y,
                                         ln2g_ref[...], ln2b_ref[...])
            yield

    tm_units = token_mixer()
    cm_units = channel_mixer()
    after = {}
    for i, u_idx in enumerate(CM_AFTER):
        after.setdefault(u_idx, []).append(i)
    n_tm = 0
    while next(tm_units, "done") != "done":
        for _ in after.pop(n_tm, []):
            next(cm_units)
        n_tm += 1
    leftover = next(cm_units, "done")
    assert not after and leftover == "done", (after, n_tm)


CM_AFTER = (0, 1, 2, 3, 4, 5, 6, 7, 8, 9, 10, 11, 12, 12, 13, 13, 14, 14)


def _resident(shape):
    return pl.BlockSpec(shape, lambda *_: (0,) * len(shape), pipeline_mode=pl.Buffered(1))


def _layer_call(x2d, tiles_per_seq, sinks, w_qvt, b_qt, b_vt, b_r, lng, lnb, w_s, bs_tile,
                ln1g, ln1b, ln2g, ln2b, w_in, w_bra, w_brg, w_out, w_up, w_down):
    n_tok, d = x2d.shape
    assert d == D_MODEL and n_tok % TM == 0 and TM % BLOCK == 0
    n_tiles = n_tok // TM
    in_hbm = pl.BlockSpec(memory_space=pl.ANY)
    return pl.pallas_call(
        functools.partial(_layer_kernel, tiles_per_seq),
        out_shape=jax.ShapeDtypeStruct(x2d.shape, F32),
        grid=(n_tiles + 1,),
        in_specs=[
            pl.BlockSpec(memory_space=pltpu.SMEM),
            pl.BlockSpec((TM, D_MODEL), lambda t: (jnp.minimum(t, n_tiles - 1), 0)),
            _resident((Q_WIDTH + KV_WIDTH, D_MODEL)),
            _resident((Q_WIDTH, TM)),
            _resident((KV_WIDTH, TM)),
            _resident((1, R_WIDTH)),
            _resident((1, GMLP_WIDTH)),
            _resident((1, GMLP_WIDTH)),
            _resident((GMLP_GROUPS, BLOCK, BLOCK)),
            _resident((BLOCK, GMLP_WIDTH)),
            _resident((1, D_MODEL)),
            _resident((1, D_MODEL)),
            _resident((1, D_MODEL)),
            _resident((1, D_MODEL)),
            in_hbm, in_hbm, in_hbm, in_hbm, in_hbm, in_hbm,
        ],
        out_specs=pl.BlockSpec((TM, D_MODEL), lambda t: (jnp.maximum(t - 1, 0), 0)),
        scratch_shapes=[
            pltpu.VMEM((2, BLOCK, KV_WIDTH), BF16),
            pltpu.VMEM((2, KV_WIDTH, BLOCK), BF16),
            pltpu.VMEM((2, TM, D_MODEL), F32),
            pltpu.VMEM((Q_WIDTH, TM), BF16),
            pltpu.VMEM((TM, GMLP_WIDTH), BF16),
            pltpu.VMEM((D_MODEL, R_WIDTH), BF16),
            pltpu.VMEM((Q_WIDTH, D_MODEL), BF16),
            pltpu.VMEM((GMLP_WIDTH, D_MODEL), BF16),
            pltpu.VMEM((D_MODEL, D_MODEL), BF16),
            pltpu.VMEM((D_MODEL, D_FF), BF16),
            pltpu.VMEM((D_FF, D_MODEL), BF16),
        ],
        compiler_params=pltpu.CompilerParams(
            dimension_semantics=("arbitrary",),
            vmem_limit_bytes=V7X_VMEM_LIMIT_BYTES),
        name="decoder_layer",
    )(sinks, x2d, w_qvt, b_qt, b_vt, b_r, lng, lnb, w_s, bs_tile, ln1g, ln1b, ln2g, ln2b,
      w_in, w_bra, w_brg, w_out, w_up, w_down)


def kernel(x, w_in, b_in, attn_sinks, gmlp_ln_g, gmlp_ln_b, gmlp_w_s, gmlp_b_s, w_branch_attn,
           w_branch_gmlp, w_out, ln1_g, ln1_b, w_up, w_down, ln2_g, ln2_b):
    assert w_in.shape[0] == DEPTH
    bsz, seq, d = x.shape
    assert seq % TM == 0
    x2d = x.reshape(bsz * seq, d)
    for l in range(DEPTH):
        bs_tile = jnp.repeat(gmlp_b_s[l].T, GMLP_GROUP_DIM, axis=1)
        w_qvt = jnp.concatenate([w_in[l][:, OFF_Q:OFF_K], w_in[l][:, OFF_V:OFF_Z]], axis=1).T
        b_qt = jnp.broadcast_to(b_in[l][OFF_Q:OFF_K, None], (Q_WIDTH, TM))
        b_vt = jnp.broadcast_to(b_in[l][OFF_V:OFF_Z, None], (KV_WIDTH, TM))
        b_r = jnp.concatenate([b_in[l][OFF_K:OFF_V], b_in[l][OFF_Z:]])[None, :]
        x2d = _layer_call(
            x2d, seq // TM, attn_sinks[l], w_qvt.astype(BF16), b_qt, b_vt, b_r,
            gmlp_ln_g[l][None, :], gmlp_ln_b[l][None, :], gmlp_w_s[l], bs_tile,
            ln1_g[l][None, :], ln1_b[l][None, :], ln2_g[l][None, :], ln2_b[l][None, :],
            w_in[l], w_branch_attn[l], w_branch_gmlp[l], w_out[l], w_up[l], w_down[l])
    return x2d.reshape(bsz, seq, d)
```

```python
import functools

import jax
import jax.numpy as jnp
from jax.experimental import pallas as pl
from jax.experimental.pallas import tpu as pltpu

D_MODEL = 1024
N_Q_HEADS = 16
N_KV_HEADS = 4
GROUP = N_Q_HEADS // N_KV_HEADS
HEAD_DIM = 64
Q_WIDTH = N_Q_HEADS * HEAD_DIM
KV_WIDTH = N_KV_HEADS * HEAD_DIM
WINDOW = 128
BLOCK = 128
GMLP_WIDTH = 1024
GMLP_GROUPS = 8
GMLP_GROUP_DIM = GMLP_WIDTH // GMLP_GROUPS
D_FF = 4 * D_MODEL
LN_EPS = 1e-5
DEPTH = 1
DEEPNORM_ALPHA = (2 * DEPTH) ** 0.25

OFF_Q = 0
OFF_K = OFF_Q + Q_WIDTH
OFF_V = OFF_K + KV_WIDTH
OFF_Z = OFF_V + KV_WIDTH
OFF_GA = OFF_Z + 2 * GMLP_WIDTH
OFF_GG = OFF_GA + D_MODEL
IN_WIDTH = OFF_GG + D_MODEL
R_K = OFF_K
R_Z = OFF_Z
R_GA = OFF_GA
R_GG = OFF_GG
R_WIDTH = IN_WIDTH

TM = 256
FF_CHUNK = 512
P_CHUNK = 256
CONVERT_ROWS, CONVERT_COLS = 256, 1024
CONVERT_SLOTS = 4
V7X_VMEM_LIMIT_BYTES = 60 * 1024 * 1024

BF16 = jnp.bfloat16
F32 = jnp.float32


def _dot(a, b):
    return jnp.dot(a, b, preferred_element_type=F32)


def _layer_norm(x, g, b):
    mu = jnp.mean(x, axis=-1, keepdims=True)
    xc = x - mu
    var = jnp.mean(xc * xc, axis=-1, keepdims=True)
    return xc * jax.lax.rsqrt(var + LN_EPS) * g + b


def _convert_weights(jobs, block_rows, block_cols):
    blocks = []
    for src_hbm, src_col0, dst_ref, dst_col0, n_cols, transposed in jobs:
        n_rows = src_hbm.shape[0]
        assert n_rows % block_rows == 0 and n_cols % block_cols == 0
        for r0 in range(0, n_rows, block_rows):
            for c0 in range(0, n_cols, block_cols):
                blocks.append((src_hbm, r0, src_col0 + c0, dst_ref, dst_col0 + c0, transposed))

    def body(stage, sem):
        def block_copy(i):
            src_hbm, r0, src_c = blocks[i][:3]
            slot = i % CONVERT_SLOTS
            return pltpu.make_async_copy(
                src_hbm.at[pl.ds(r0, block_rows), pl.ds(src_c, block_cols)],
                stage.at[slot], sem.at[slot])

        for i in range(min(CONVERT_SLOTS - 1, len(blocks))):
            block_copy(i).start()
        for i, (_, r0, _, dst_ref, dst_c, transposed) in enumerate(blocks):
            block_copy(i).wait()
            block = stage[i % CONVERT_SLOTS]
            if transposed:
                dst_ref[dst_c:dst_c + block_cols, r0:r0 + block_rows] = block.T.astype(BF16)
            else:
                dst_ref[r0:r0 + block_rows, dst_c:dst_c + block_cols] = block.astype(BF16)
            nxt = i + CONVERT_SLOTS - 1
            if nxt < len(blocks):
                block_copy(nxt).start()

    pl.run_scoped(body, pltpu.VMEM((CONVERT_SLOTS, block_rows, block_cols), F32),
                  pltpu.SemaphoreType.DMA((CONVERT_SLOTS,)))


def _layer_kernel(tiles_per_seq,
                  sinks_ref, x_ref, b_qt_ref, b_vt_ref, b_r_ref,
                  lng_ref, lnb_ref, ws_ref, bs_ref, ln1g_ref, ln1b_ref, ln2g_ref, ln2b_ref,
                  w_in_hbm, w_bra_hbm, w_brg_hbm, w_out_hbm, w_up_hbm, w_down_hbm,
                  o_ref, kprev_ref, vprev_ref, resid_ref, attn_ref, sg_ref,
                  w_qvt_ref, w_r_ref, w_bra_ref, w_brg_ref, w_out_ref, w_up_ref, w_down_ref):
    t = pl.program_id(0)

    @pl.when(t == 0)
    def _():
        _convert_weights([
            (w_in_hbm, OFF_K, w_r_ref, R_K, KV_WIDTH, False),
            (w_in_hbm, OFF_Q, w_qvt_ref, 0, Q_WIDTH, True),
            (w_in_hbm, OFF_V, w_qvt_ref, Q_WIDTH, KV_WIDTH, True),
        ], D_MODEL, KV_WIDTH)
        _convert_weights([
            (w_in_hbm, OFF_Z, w_r_ref, R_Z, IN_WIDTH - OFF_Z, False),
            (w_bra_hbm, 0, w_bra_ref, 0, D_MODEL, False),
            (w_brg_hbm, 0, w_brg_ref, 0, D_MODEL, False),
            (w_out_hbm, 0, w_out_ref, 0, D_MODEL, False),
            (w_up_hbm, 0, w_up_ref, 0, D_FF, False),
            (w_down_hbm, 0, w_down_ref, 0, D_MODEL, False),
        ], CONVERT_ROWS, CONVERT_COLS)

    n_blk = TM // BLOCK
    rd_slot = (t + 1) % 2
    wr_slot = t % 2
    has_prev_tile = (t % tiles_per_seq) > 0

    def token_mixer():
        x = x_ref[...]
        xb = x.astype(BF16)

        def proj(lo, hi):
            return _dot(xb, w_r_ref[:, lo:hi]) + b_r_ref[:, lo:hi]

        def gelu_chunk(c):
            return jax.nn.gelu(proj(R_Z + c * P_CHUNK, R_Z + (c + 1) * P_CHUNK), approximate=True)

        qv_t = jax.lax.dot_general(w_qvt_ref[...], xb, (((1,), (1,)), ((), ())),
                                   preferred_element_type=F32)
        q_t = ((qv_t[:Q_WIDTH, :] + b_qt_ref[...]) * (HEAD_DIM ** -0.5)).astype(BF16)
        v_t_new = (qv_t[Q_WIDTH:, :] + b_vt_ref[...]).astype(BF16)
        yield

        k_new = proj(R_K, R_K + KV_WIDTH).astype(BF16)
        k_prev = jnp.where(has_prev_tile, kprev_ref[rd_slot], jnp.zeros((BLOCK, KV_WIDTH), BF16))
        v_t_prev = jnp.where(has_prev_tile, vprev_ref[rd_slot], jnp.zeros((KV_WIDTH, BLOCK), BF16))
        k_all = jnp.concatenate([k_prev, k_new], axis=0)
        v_t_all = jnp.concatenate([v_t_prev, v_t_new], axis=1)
        kprev_ref[wr_slot] = k_new[TM - BLOCK:, :]
        vprev_ref[wr_slot] = v_t_new[:, TM - BLOCK:]
        n_z = 2 * GMLP_WIDTH // P_CHUNK
        n_lead = n_z // 2
        zs = [gelu_chunk(c) for c in range(n_lead // 2)]
        yield
        zs += [gelu_chunk(c) for c in range(n_lead // 2, n_lead)]
        yield

        k_pos = jax.lax.broadcasted_iota(jnp.int32, (2 * BLOCK, BLOCK), 0)
        q_pos = jax.lax.broadcasted_iota(jnp.int32, (2 * BLOCK, BLOCK), 1)
        diff = k_pos - q_pos
        band = (diff >= 1) & (diff <= WINDOW)
        pair_width = 2 * HEAD_DIM
        zero_rows = jnp.zeros((HEAD_DIM, GROUP * BLOCK), BF16)

        def scores_t(j, hk):
            pair, odd = divmod(hk, 2)
            toks = slice(j * BLOCK, (j + 1) * BLOCK)
            k_pair = k_all[j * BLOCK:(j + 2) * BLOCK, pair * pair_width:(pair + 1) * pair_width]
            q_cat = jnp.concatenate(
                [q_t[(hk * GROUP + g) * HEAD_DIM:(hk * GROUP + g + 1) * HEAD_DIM, toks]
                 for g in range(GROUP)], axis=1)
            rhs = jnp.concatenate([zero_rows, q_cat] if odd else [q_cat, zero_rows], axis=0)
            return _dot(k_pair, rhs)

        def softmax(j, hk, sc_t):
            first_key = jnp.where(has_prev_tile, 0, BLOCK) if j == 0 else 0
            mask = band & (k_pos >= first_key)
            toks = slice(j * BLOCK, (j + 1) * BLOCK)
            p_parts, inv_parts = [], []
            for g in range(GROUP):
                sc = jnp.where(mask, sc_t[:, g * BLOCK:(g + 1) * BLOCK], -jnp.inf)
                sink = sinks_ref[hk * GROUP + g]
                m = jnp.maximum(jnp.max(sc, axis=0, keepdims=True), sink)
                p = jnp.exp(sc - m)
                denom = jnp.sum(p, axis=0, keepdims=True) + jnp.exp(sink - m)
                p_parts.append(p.astype(BF16))
                inv_parts.append(1.0 / denom)
            return jnp.concatenate(p_parts, axis=1), inv_parts

        def pv(j, hk, p_t, inv_parts):
            toks = slice(j * BLOCK, (j + 1) * BLOCK)
            v_t = v_t_all[hk * HEAD_DIM:(hk + 1) * HEAD_DIM, j * BLOCK:(j + 2) * BLOCK]
            o_t = _dot(v_t, p_t)
            for g in range(GROUP):
                feat = slice((hk * GROUP + g) * HEAD_DIM, (hk * GROUP + g + 1) * HEAD_DIM)
                attn_ref[feat, toks] = (o_t[:, g * BLOCK:(g + 1) * BLOCK]
                                        * inv_parts[g]).astype(BF16)

        side_work = [lambda c=c: gelu_chunk(c) for c in range(n_lead, n_z)]
        side_work += [lambda c=c: proj(R_GA + c * P_CHUNK, R_GA + (c + 1) * P_CHUNK)
                      for c in range(D_MODEL // P_CHUNK)]
        items = [(j, hk) for j in range(n_blk) for hk in range(N_KV_HEADS)]
        assert len(side_work) == len(items)
        side_out = []
        sc = [scores_t(*items[0]), scores_t(*items[1])]
        for i, item in enumerate(items):
            yield
            probs = softmax(*item, sc[i])
            side_out.append(side_work[i]())
            pv(*item, *probs)
            if i + 2 < len(items):
                sc.append(scores_t(*items[i + 2]))
        yield
        zs += side_out[:n_z - n_lead]
        u = jnp.concatenate(zs[:n_z // 2], axis=1)
        vz = jnp.concatenate(zs[n_z // 2:], axis=1)
        pre_a = jnp.concatenate(side_out[n_z - n_lead:], axis=1)

        pre_g = proj(R_GG, R_WIDTH)
        v = _layer_norm(vz, lng_ref[...], lnb_ref[...]).astype(BF16)
        g_a = jax.nn.sigmoid(pre_a)
        yield

        y_a = jax.lax.dot_general(attn_ref[...], w_bra_ref[...], (((0,), (0,)), ((), ())),
                                  preferred_element_type=F32)
        t_idx = jax.lax.broadcasted_iota(jnp.int32, (BLOCK, BLOCK), 0)
        s_idx = jax.lax.broadcasted_iota(jnp.int32, (BLOCK, BLOCK), 1)
        causal = s_idx <= t_idx
        for g in range(GMLP_GROUPS):
            w = jnp.where(causal, ws_ref[g], 0.0).astype(BF16)
            cols = slice(g * GMLP_GROUP_DIM, (g + 1) * GMLP_GROUP_DIM)
            for c in range(n_blk):
                rows = slice(c * BLOCK, (c + 1) * BLOCK)
                mixed = _dot(w, v[rows, cols]) + bs_ref[:, cols]
                sg_ref[rows, cols] = (u[rows, cols] * mixed).astype(BF16)
        g_g = jax.nn.sigmoid(pre_g)
        mix = g_a * y_a
        yield

        y_g = _dot(sg_ref[...], w_brg_ref[...])
        mix = (mix + g_g * y_g).astype(BF16)
        yield
        resid_ref[wr_slot] = DEEPNORM_ALPHA * x + _dot(mix, w_out_ref[...])
        yield

    def channel_mixer():
        resid = jnp.where(t > 0, resid_ref[rd_slot], 0.0)
        x1 = _layer_norm(resid, ln1g_ref[...], ln1b_ref[...])
        x1b = x1.astype(BF16)
        yield
        n_chunks = D_FF // FF_CHUNK
        acc = None
        for c in range(n_chunks):
            cols = slice(c * FF_CHUNK, (c + 1) * FF_CHUNK)
            h = jnp.maximum(_dot(x1b, w_up_ref[:, cols]), 0.0)
            h = (h * h).astype(BF16)
            yield
            if c < n_chunks - 1:
                part = _dot(h, w_down_ref[cols, :])
                acc = part if acc is None else acc + part
                yield
        for r in range(n_blk):
            rows = slice(r * BLOCK, (r + 1) * BLOCK)
            y = acc[rows, :] + _dot(h[rows, :], w_down_ref[cols, :])
            o_ref[rows, :] = _layer_norm(DEEPNORM_ALPHA * x1[rows, :] + y,
                                         ln2g_ref[...], ln2b_ref[...])
            yield

    tm_units = token_mixer()
    cm_units = channel_mixer()
    after = {}
    for i, u_idx in enumerate(CM_AFTER):
        after.setdefault(u_idx, []).append(i)
    n_tm = 0
    while next(tm_units, "done") != "done":
        for _ in after.pop(n_tm, []):
            next(cm_units)
        n_tm += 1
    leftover = next(cm_units, "done")
    assert not after and leftover == "done", (after, n_tm)


CM_AFTER = (0, 1, 2, 3, 4, 5, 6, 7, 8, 9, 10, 11, 12, 12, 13, 13, 14, 14)


def _resident(shape):
    return pl.BlockSpec(shape, lambda *_: (0,) * len(shape), pipeline_mode=pl.Buffered(1))


def _layer_call(x2d, tiles_per_seq, sinks, b_qt, b_vt, b_r, lng, lnb, w_s, bs_tile,
                ln1g, ln1b, ln2g, ln2b, w_in, w_bra, w_brg, w_out, w_up, w_down):
    n_tok, d = x2d.shape
    assert d == D_MODEL and n_tok % TM == 0 and TM % BLOCK == 0
    n_tiles = n_tok // TM
    in_hbm = pl.BlockSpec(memory_space=pl.ANY)
    return pl.pallas_call(
        functools.partial(_layer_kernel, tiles_per_seq),
        out_shape=jax.ShapeDtypeStruct(x2d.shape, F32),
        grid=(n_tiles + 1,),
        in_specs=[
            pl.BlockSpec(memory_space=pltpu.SMEM),
            pl.BlockSpec((TM, D_MODEL), lambda t: (jnp.minimum(t, n_tiles - 1), 0)),
            _resident((Q_WIDTH, TM)),
            _resident((KV_WIDTH, TM)),
            _resident((1, R_WIDTH)),
            _resident((1, GMLP_WIDTH)),
            _resident((1, GMLP_WIDTH)),
            _resident((GMLP_GROUPS, BLOCK, BLOCK)),
            _resident((BLOCK, GMLP_WIDTH)),
            _resident((1, D_MODEL)),
            _resident((1, D_MODEL)),
            _resident((1, D_MODEL)),
            _resident((1, D_MODEL)),
            in_hbm, in_hbm, in_hbm, in_hbm, in_hbm, in_hbm,
        ],
        out_specs=pl.BlockSpec((TM, D_MODEL), lambda t: (jnp.maximum(t - 1, 0), 0)),
        scratch_shapes=[
            pltpu.VMEM((2, BLOCK, KV_WIDTH), BF16),
            pltpu.VMEM((2, KV_WIDTH, BLOCK), BF16),
            pltpu.VMEM((2, TM, D_MODEL), F32),
            pltpu.VMEM((Q_WIDTH, TM), BF16),
            pltpu.VMEM((TM, GMLP_WIDTH), BF16),
            pltpu.VMEM((Q_WIDTH + KV_WIDTH, D_MODEL), BF16),
            pltpu.VMEM((D_MODEL, IN_WIDTH), BF16),
            pltpu.VMEM((Q_WIDTH, D_MODEL), BF16),
            pltpu.VMEM((GMLP_WIDTH, D_MODEL), BF16),
            pltpu.VMEM((D_MODEL, D_MODEL), BF16),
            pltpu.VMEM((D_MODEL, D_FF), BF16),
            pltpu.VMEM((D_FF, D_MODEL), BF16),
        ],
        compiler_params=pltpu.CompilerParams(
            dimension_semantics=("arbitrary",),
            vmem_limit_bytes=V7X_VMEM_LIMIT_BYTES),
        name="decoder_layer",
    )(sinks, x2d, b_qt, b_vt, b_r, lng, lnb, w_s, bs_tile, ln1g, ln1b, ln2g, ln2b,
      w_in, w_bra, w_brg, w_out, w_up, w_down)


def kernel(x, w_in, b_in, attn_sinks, gmlp_ln_g, gmlp_ln_b, gmlp_w_s, gmlp_b_s, w_branch_attn,
           w_branch_gmlp, w_out, ln1_g, ln1_b, w_up, w_down, ln2_g, ln2_b):
    assert w_in.shape[0] == DEPTH
    bsz, seq, d = x.shape
    assert seq % TM == 0
    x2d = x.reshape(bsz * seq, d)
    for l in range(DEPTH):
        bs_tile = jnp.repeat(gmlp_b_s[l].T, GMLP_GROUP_DIM, axis=1)
        b_qt = jnp.broadcast_to(b_in[l][OFF_Q:OFF_K, None], (Q_WIDTH, TM))
        b_vt = jnp.broadcast_to(b_in[l][OFF_V:OFF_Z, None], (KV_WIDTH, TM))
        b_r = b_in[l][None, :]
        x2d = _layer_call(
            x2d, seq // TM, attn_sinks[l], b_qt, b_vt, b_r,
            gmlp_ln_g[l][None, :], gmlp_ln_b[l][None, :], gmlp_w_s[l], bs_tile,
            ln1_g[l][None, :], ln1_b[l][None, :], ln2_g[l][None, :], ln2_b[l][None, :],
            w_in[l], w_branch_attn[l], w_branch_gmlp[l], w_out[l], w_up[l], w_down[l])
    return x2d.reshape(bsz, seq, d)
```

```python
import functools

import jax
import jax.numpy as jnp
from jax.experimental import pallas as pl
from jax.experimental.pallas import tpu as pltpu

D_MODEL = 1024
N_Q_HEADS = 16
N_KV_HEADS = 4
GROUP = N_Q_HEADS // N_KV_HEADS
HEAD_DIM = 64
Q_WIDTH = N_Q_HEADS * HEAD_DIM
KV_WIDTH = N_KV_HEADS * HEAD_DIM
WINDOW = 128
BLOCK = 128
GMLP_WIDTH = 1024
GMLP_GROUPS = 8
GMLP_GROUP_DIM = GMLP_WIDTH // GMLP_GROUPS
D_FF = 4 * D_MODEL
LN_EPS = 1e-5
DEPTH = 1
DEEPNORM_ALPHA = (2 * DEPTH) ** 0.25

OFF_Q = 0
OFF_K = OFF_Q + Q_WIDTH
OFF_V = OFF_K + KV_WIDTH
OFF_Z = OFF_V + KV_WIDTH
OFF_GA = OFF_Z + 2 * GMLP_WIDTH
OFF_GG = OFF_GA + D_MODEL
IN_WIDTH = OFF_GG + D_MODEL
R_K = OFF_K
R_Z = OFF_Z
R_GA = OFF_GA
R_GG = OFF_GG
R_WIDTH = IN_WIDTH

TM = 256
FF_CHUNK = 512
P_CHUNK = 256
CONVERT_ROWS, CONVERT_COLS = 256, 1024
CONVERT_SLOTS = 4
V7X_VMEM_LIMIT_BYTES = 60 * 1024 * 1024

BF16 = jnp.bfloat16
F32 = jnp.float32


def _dot(a, b):
    return jnp.dot(a, b, preferred_element_type=F32)


def _layer_norm(x, g, b):
    mu = jnp.mean(x, axis=-1, keepdims=True)
    xc = x - mu
    var = jnp.mean(xc * xc, axis=-1, keepdims=True)
    return xc * jax.lax.rsqrt(var + LN_EPS) * g + b


def _convert_weights(jobs, block_rows, block_cols):
    blocks = []
    for src_hbm, src_col0, dst_ref, dst_col0, n_cols, transposed in jobs:
        n_rows = src_hbm.shape[0]
        assert n_rows % block_rows == 0 and n_cols % block_cols == 0
        for r0 in range(0, n_rows, block_rows):
            for c0 in range(0, n_cols, block_cols):
                blocks.append((src_hbm, r0, src_col0 + c0, dst_ref, dst_col0 + c0, transposed))

    def body(stage, sem):
        def block_copy(i):
            src_hbm, r0, src_c = blocks[i][:3]
            slot = i % CONVERT_SLOTS
            return pltpu.make_async_copy(
                src_hbm.at[pl.ds(r0, block_rows), pl.ds(src_c, block_cols)],
                stage.at[slot], sem.at[slot])

        for i in range(min(CONVERT_SLOTS - 1, len(blocks))):
            block_copy(i).start()
        for i, (_, r0, _, dst_ref, dst_c, transposed) in enumerate(blocks):
            block_copy(i).wait()
            block = stage[i % CONVERT_SLOTS]
            if transposed:
                dst_ref[dst_c:dst_c + block_cols, r0:r0 + block_rows] = block.T.astype(BF16)
            else:
                dst_ref[r0:r0 + block_rows, dst_c:dst_c + block_cols] = block.astype(BF16)
            nxt = i + CONVERT_SLOTS - 1
            if nxt < len(blocks):
                block_copy(nxt).start()

    pl.run_scoped(body, pltpu.VMEM((CONVERT_SLOTS, block_rows, block_cols), F32),
                  pltpu.SemaphoreType.DMA((CONVERT_SLOTS,)))


def _layer_kernel(tiles_per_seq,
                  sinks_ref, x_ref, b_qt_ref, b_vt_ref, b_r_ref,
                  lng_ref, lnb_ref, ws_ref, bs_ref, ln1g_ref, ln1b_ref, ln2g_ref, ln2b_ref,
                  w_in_hbm, w_bra_hbm, w_brg_hbm, w_out_hbm, w_up_hbm, w_down_hbm,
                  o_ref, kprev_ref, vprev_ref, resid_ref, attn_ref, sg_ref,
                  w_qvt_ref, w_r_ref, w_bra_ref, w_brg_ref, w_out_ref, w_up_ref, w_down_ref):
    t = pl.program_id(0)

    @pl.when(t == 0)
    def _():
        _convert_weights([
            (w_in_hbm, OFF_K, w_r_ref, R_K, KV_WIDTH, False),
            (w_in_hbm, OFF_Q, w_qvt_ref, 0, Q_WIDTH, True),
            (w_in_hbm, OFF_V, w_qvt_ref, Q_WIDTH, KV_WIDTH, True),
        ], D_MODEL, KV_WIDTH)
        _convert_weights([
            (w_in_hbm, OFF_Z, w_r_ref, R_Z, IN_WIDTH - OFF_Z, False),
            (w_bra_hbm, 0, w_bra_ref, 0, D_MODEL, False),
            (w_brg_hbm, 0, w_brg_ref, 0, D_MODEL, False),
            (w_out_hbm, 0, w_out_ref, 0, D_MODEL, False),
            (w_up_hbm, 0, w_up_ref, 0, D_FF, False),
            (w_down_hbm, 0, w_down_ref, 0, D_MODEL, False),
        ], CONVERT_ROWS, CONVERT_COLS)

    n_blk = TM // BLOCK
    rd_slot = (t + 1) % 2
    wr_slot = t % 2
    has_prev_tile = (t % tiles_per_seq) > 0

    def token_mixer():
        x = x_ref[...]
        xb = x.astype(BF16)

        def proj(lo, hi):
            return _dot(xb, w_r_ref[:, lo:hi]) + b_r_ref[:, lo:hi]

        def gelu_chunk(c):
            return jax.nn.gelu(proj(R_Z + c * P_CHUNK, R_Z + (c + 1) * P_CHUNK), approximate=True)

        qv_t = jax.lax.dot_general(w_qvt_ref[...], xb, (((1,), (1,)), ((), ())),
                                   preferred_element_type=F32)
        q_t = ((qv_t[:Q_WIDTH, :] + b_qt_ref[...]) * (HEAD_DIM ** -0.5)).astype(BF16)
        v_t_new = (qv_t[Q_WIDTH:, :] + b_vt_ref[...]).astype(BF16)
        yield

        k_new = proj(R_K, R_K + KV_WIDTH).astype(BF16)
        k_prev = jnp.where(has_prev_tile, kprev_ref[rd_slot], jnp.zeros((BLOCK, KV_WIDTH), BF16))
        v_t_prev = jnp.where(has_prev_tile, vprev_ref[rd_slot], jnp.zeros((KV_WIDTH, BLOCK), BF16))
        k_all = jnp.concatenate([k_prev, k_new], axis=0)
        v_t_all = jnp.concatenate([v_t_prev, v_t_new], axis=1)
        kprev_ref[wr_slot] = k_new[TM - BLOCK:, :]
        vprev_ref[wr_slot] = v_t_new[:, TM - BLOCK:]
        n_z = 2 * GMLP_WIDTH // P_CHUNK
        n_lead = n_z // 2
        zs = [gelu_chunk(c) for c in range(n_lead // 2)]
        yield
        zs += [gelu_chunk(c) for c in range(n_lead // 2, n_lead)]
        yield

        k_pos = jax.lax.broadcasted_iota(jnp.int32, (2 * BLOCK, BLOCK), 0)
        q_pos = jax.lax.broadcasted_iota(jnp.int32, (2 * BLOCK, BLOCK), 1)
        diff = k_pos - q_pos
        band = (diff >= 1) & (diff <= WINDOW)
        pair_width = 2 * HEAD_DIM
        zero_rows = jnp.zeros((HEAD_DIM, GROUP * BLOCK), BF16)

        def scores_t(j, hk):
            pair, odd = divmod(hk, 2)
            toks = slice(j * BLOCK, (j + 1) * BLOCK)
            k_pair = k_all[j * BLOCK:(j + 2) * BLOCK, pair * pair_width:(pair + 1) * pair_width]
            q_cat = jnp.concatenate(
                [q_t[(hk * GROUP + g) * HEAD_DIM:(hk * GROUP + g + 1) * HEAD_DIM, toks]
                 for g in range(GROUP)], axis=1)
            rhs = jnp.concatenate([zero_rows, q_cat] if odd else [q_cat, zero_rows], axis=0)
            return _dot(k_pair, rhs)

        def softmax(j, hk, sc_t):
            first_key = jnp.where(has_prev_tile, 0, BLOCK) if j == 0 else 0
            mask = band & (k_pos >= first_key)
            toks = slice(j * BLOCK, (j + 1) * BLOCK)
            p_parts, inv_parts = [], []
            for g in range(GROUP):
                sc = jnp.where(mask, sc_t[:, g * BLOCK:(g + 1) * BLOCK], -jnp.inf)
                sink = sinks_ref[hk * GROUP + g]
                m = jnp.maximum(jnp.max(sc, axis=0, keepdims=True), sink)
                p = jnp.exp(sc - m)
                denom = jnp.sum(p, axis=0, keepdims=True) + jnp.exp(sink - m)
                p_parts.append(p.astype(BF16))
                inv_parts.append(1.0 / denom)
            return jnp.concatenate(p_parts, axis=1), inv_parts

        def pv(j, hk, p_t, inv_parts):
            toks = slice(j * BLOCK, (j + 1) * BLOCK)
            v_t = v_t_all[hk * HEAD_DIM:(hk + 1) * HEAD_DIM, j * BLOCK:(j + 2) * BLOCK]
            o_t = _dot(v_t, p_t)
            for g in range(GROUP):
                feat = slice((hk * GROUP + g) * HEAD_DIM, (hk * GROUP + g + 1) * HEAD_DIM)
                attn_ref[feat, toks] = (o_t[:, g * BLOCK:(g + 1) * BLOCK]
                                        * inv_parts[g]).astype(BF16)

        side_work = [lambda c=c: gelu_chunk(c) for c in range(n_lead, n_z)]
        side_work += [lambda c=c: jax.nn.sigmoid(proj(R_GA + c * P_CHUNK, R_GA + (c + 1) * P_CHUNK))
                      for c in range(D_MODEL // P_CHUNK)]
        items = [(j, hk) for j in range(n_blk) for hk in range(N_KV_HEADS)]
        assert len(side_work) == len(items)
        side_out = []
        sc = [scores_t(*items[0]), scores_t(*items[1])]
        for i, item in enumerate(items):
            yield
            probs = softmax(*item, sc[i])
            side_out.append(side_work[i]())
            pv(*item, *probs)
            if i + 2 < len(items):
                sc.append(scores_t(*items[i + 2]))
        yield
        zs += side_out[:n_z - n_lead]
        u = jnp.concatenate(zs[:n_z // 2], axis=1)
        vz = jnp.concatenate(zs[n_z // 2:], axis=1)
        g_a = jnp.concatenate(side_out[n_z - n_lead:], axis=1)

        g_g = jnp.concatenate(
            [jax.nn.sigmoid(proj(R_GG + c * P_CHUNK, R_GG + (c + 1) * P_CHUNK))
             for c in range(D_MODEL // P_CHUNK)], axis=1)
        v = _layer_norm(vz, lng_ref[...], lnb_ref[...]).astype(BF16)
        yield

        y_a = jax.lax.dot_general(attn_ref[...], w_bra_ref[...], (((0,), (0,)), ((), ())),
                                  preferred_element_type=F32)
        t_idx = jax.lax.broadcasted_iota(jnp.int32, (BLOCK, BLOCK), 0)
        s_idx = jax.lax.broadcasted_iota(jnp.int32, (BLOCK, BLOCK), 1)
        causal = s_idx <= t_idx
        for g in range(GMLP_GROUPS):
            w = jnp.where(causal, ws_ref[g], 0.0).astype(BF16)
            cols = slice(g * GMLP_GROUP_DIM, (g + 1) * GMLP_GROUP_DIM)
            v_g = jnp.concatenate([v[c * BLOCK:(c + 1) * BLOCK, cols] for c in range(n_blk)],
                                  axis=1)
            mixed_g = _dot(w, v_g)
            for c in range(n_blk):
                rows = slice(c * BLOCK, (c + 1) * BLOCK)
                mixed = mixed_g[:, c * BLOCK:(c + 1) * BLOCK] + bs_ref[:, cols]
                sg_ref[rows, cols] = (u[rows, cols] * mixed).astype(BF16)
        mix = g_a * y_a
        yield

        y_g = _dot(sg_ref[...], w_brg_ref[...])
        mix = (mix + g_g * y_g).astype(BF16)
        yield
        resid_ref[wr_slot] = DEEPNORM_ALPHA * x + _dot(mix, w_out_ref[...])
        yield

    def channel_mixer():
        resid = jnp.where(t > 0, resid_ref[rd_slot], 0.0)
        x1 = _layer_norm(resid, ln1g_ref[...], ln1b_ref[...])
        x1b = x1.astype(BF16)
        yield
        n_chunks = D_FF // FF_CHUNK
        acc = None
        for c in range(n_chunks):
            cols = slice(c * FF_CHUNK, (c + 1) * FF_CHUNK)
            h = jnp.maximum(_dot(x1b, w_up_ref[:, cols]), 0.0)
            h = (h * h).astype(BF16)
            yield
            if c < n_chunks - 1:
                part = _dot(h, w_down_ref[cols, :])
                acc = part if acc is None else acc + part
                yield
        for r in range(n_blk):
            rows = slice(r * BLOCK, (r + 1) * BLOCK)
            y = acc[rows, :] + _dot(h[rows, :], w_down_ref[cols, :])
            o_ref[rows, :] = _layer_norm(DEEPNORM_ALPHA * x1[rows, :] + y,
                                         ln2g_ref[...], ln2b_ref[...])
            yield

    tm_units = token_mixer()
    cm_units = channel_mixer()
    after = {}
    for i, u_idx in enumerate(CM_AFTER):
        after.setdefault(u_idx, []).append(i)
    n_tm = 0
    while next(tm_units, "done") != "done":
        for _ in after.pop(n_tm, []):
            next(cm_units)
        n_tm += 1
    leftover = next(cm_units, "done")
    assert not after and leftover == "done", (after, n_tm)


CM_AFTER = (0, 1, 2, 3, 4, 5, 6, 7, 8, 9, 10, 11, 12, 12, 13, 13, 14, 14)


def _resident(shape):
    return pl.BlockSpec(shape, lambda *_: (0,) * len(shape), pipeline_mode=pl.Buffered(1))


def _layer_call(x2d, tiles_per_seq, sinks, b_qt, b_vt, b_r, lng, lnb, w_s, bs_tile,
                ln1g, ln1b, ln2g, ln2b, w_in, w_bra, w_brg, w_out, w_up, w_down):
    n_tok, d = x2d.shape
    assert d == D_MODEL and n_tok % TM == 0 and TM % BLOCK == 0
    n_tiles = n_tok // TM
    in_hbm = pl.BlockSpec(memory_space=pl.ANY)
    return pl.pallas_call(
        functools.partial(_layer_kernel, tiles_per_seq),
        out_shape=jax.ShapeDtypeStruct(x2d.shape, F32),
        grid=(n_tiles + 1,),
        in_specs=[
            pl.BlockSpec(memory_space=pltpu.SMEM),
            pl.BlockSpec((TM, D_MODEL), lambda t: (jnp.minimum(t, n_tiles - 1), 0)),
            _resident((Q_WIDTH, TM)),
            _resident((KV_WIDTH, TM)),
            _resident((1, R_WIDTH)),
            _resident((1, GMLP_WIDTH)),
            _resident((1, GMLP_WIDTH)),
            _resident((GMLP_GROUPS, BLOCK, BLOCK)),
            _resident((BLOCK, GMLP_WIDTH)),
            _resident((1, D_MODEL)),
            _resident((1, D_MODEL)),
            _resident((1, D_MODEL)),
            _resident((1, D_MODEL)),
            in_hbm, in_hbm, in_hbm, in_hbm, in_hbm, in_hbm,
        ],
        out_specs=pl.BlockSpec((TM, D_MODEL), lambda t: (jnp.maximum(t - 1, 0), 0)),
        scratch_shapes=[
            pltpu.VMEM((2, BLOCK, KV_WIDTH), BF16),
            pltpu.VMEM((2, KV_WIDTH, BLOCK), BF16),
            pltpu.VMEM((2, TM, D_MODEL), F32),
            pltpu.VMEM((Q_WIDTH, TM), BF16),
            pltpu.VMEM((TM, GMLP_WIDTH), BF16),
            pltpu.VMEM((Q_WIDTH + KV_WIDTH, D_MODEL), BF16),
            pltpu.VMEM((D_MODEL, IN_WIDTH), BF16),
            pltpu.VMEM((Q_WIDTH, D_MODEL), BF16),
            pltpu.VMEM((GMLP_WIDTH, D_MODEL), BF16),
            pltpu.VMEM((D_MODEL, D_MODEL), BF16),
            pltpu.VMEM((D_MODEL, D_FF), BF16),
            pltpu.VMEM((D_FF, D_MODEL), BF16),
        ],
        compiler_params=pltpu.CompilerParams(
            dimension_semantics=("arbitrary",),
            vmem_limit_bytes=V7X_VMEM_LIMIT_BYTES),
        name="decoder_layer",
    )(sinks, x2d, b_qt, b_vt, b_r, lng, lnb, w_s, bs_tile, ln1g, ln1b, ln2g, ln2b,
      w_in, w_bra, w_brg, w_out, w_up, w_down)


def kernel(x, w_in, b_in, attn_sinks, gmlp_ln_g, gmlp_ln_b, gmlp_w_s, gmlp_b_s, w_branch_attn,
           w_branch_gmlp, w_out, ln1_g, ln1_b, w_up, w_down, ln2_g, ln2_b):
    assert w_in.shape[0] == DEPTH
    bsz, seq, d = x.shape
    assert seq % TM == 0
    x2d = x.reshape(bsz * seq, d)
    for l in range(DEPTH):
        bs_tile = jnp.repeat(gmlp_b_s[l].T, GMLP_GROUP_DIM, axis=1)
        b_qt = jnp.broadcast_to(b_in[l][OFF_Q:OFF_K, None], (Q_WIDTH, TM))
        b_vt = jnp.broadcast_to(b_in[l][OFF_V:OFF_Z, None], (KV_WIDTH, TM))
        b_r = b_in[l][None, :]
        x2d = _layer_call(
            x2d, seq // TM, attn_sinks[l], b_qt, b_vt, b_r,
            gmlp_ln_g[l][None, :], gmlp_ln_b[l][None, :], gmlp_w_s[l], bs_tile,
            ln1_g[l][None, :], ln1_b[l][None, :], ln2_g[l][None, :], ln2_b[l][None, :],
            w_in[l], w_branch_attn[l], w_branch_gmlp[l], w_out[l], w_up[l], w_down[l])
    return x2d.reshape(bsz, seq, d)
```
